```python
import math
import jax, jax.numpy as jnp
from jax import lax
import numpy as np

D_MODEL = 1024
BATCH = 4
SEQ = 8192
DEPTH = 4

GRID_W = 64
CTX_LEN = 256
N_MIXERS = 3
N_MOD = 6
NORM_EPS = 1e-6
NEG_INF = -1e30
A_HEADS = 16
A_KV_HEADS = 4
A_HEAD_DIM = D_MODEL // A_HEADS
A_WINDOW = 128
A_BLOCK = 128
ROPE_BASE = 10000.0
M_HEADS = 4
M_V_DIM = D_MODEL // M_HEADS
M_QK_DIM = M_V_DIM // 2
M_CHUNK = 64
C_HEADS = 16
C_HEAD_DIM = D_MODEL // C_HEADS
NA_WIN_R = 8
NA_WIN_C = 16
N_EXPERTS = 32
TOP_K = 4
D_FF = D_MODEL
SWIGLU_LIMIT = 7.0
SWIGLU_ALPHA = 1.702
MOE_BLOCK = 512
N_LAYERS_A = (DEPTH + N_MIXERS - 1) // N_MIXERS
N_LAYERS_B = (DEPTH + N_MIXERS - 2) // N_MIXERS
N_LAYERS_C = DEPTH // N_MIXERS

kernel_name = 'hybrid_dit_swa_mlstm_natten_moe'

F32 = jnp.float32


def rms_norm(x, g):
    x32 = x.astype(F32)
    y = x32 * lax.rsqrt(jnp.mean(x32 * x32, axis=-1, keepdims=True) + NORM_EPS)
    return (y * g.astype(F32)).astype(x.dtype)


def modulate(h, shift, scale):
    return h * (1 + scale) + shift


def axial_rope(t, rows, cols):
    d = t.shape[-1]
    half = d // 2
    quarter = half // 2
    freqs = ROPE_BASE ** (-jnp.arange(quarter, dtype=F32) / quarter)

    def rot(u, pos):
        ang = pos.astype(F32)[:, None] * freqs[None, :]
        cos = jnp.cos(ang)[None, :, None, :]
        sin = jnp.sin(ang)[None, :, None, :]
        u1 = u[..., :quarter].astype(F32)
        u2 = u[..., quarter:].astype(F32)
        return jnp.concatenate([u1 * cos - u2 * sin, u1 * sin + u2 * cos], axis=-1)

    out = jnp.concatenate([rot(t[..., :half], rows), rot(t[..., half:], cols)], axis=-1)
    return out.astype(t.dtype)


def window_gqa(hl, hc, wqkv, wo, sink, rows, cols, need_ctx):
    B, S, _ = hl.shape
    Lc = hc.shape[1]
    G = A_HEADS // A_KV_HEADS
    qw = A_HEADS * A_HEAD_DIM
    kw = A_KV_HEADS * A_HEAD_DIM
    scale = A_HEAD_DIM ** -0.5
    sink_g = sink.astype(F32).reshape(A_KV_HEADS, G)

    def q_proj(h):
        return (h @ wqkv[:, :qw]).reshape(B, h.shape[1], A_HEADS, A_HEAD_DIM)

    def kv_proj(h):
        p = h @ wqkv[:, qw:]
        L = h.shape[1]
        return (p[..., :kw].reshape(B, L, A_KV_HEADS, A_HEAD_DIM),
                p[..., kw:].reshape(B, L, A_KV_HEADS, A_HEAD_DIM))

    ql = axial_rope(q_proj(hl), rows, cols).reshape(B, S, A_KV_HEADS, G, A_HEAD_DIM)
    kl, vl = kv_proj(hl)
    kl = axial_rope(kl, rows, cols)
    kc, vc = kv_proj(hc)
    pad = ((0, 0), (A_BLOCK, A_BLOCK), (0, 0), (0, 0))
    kp = jnp.pad(kl, pad)
    vp = jnp.pad(vl, pad)
    n_blocks = S // A_BLOCK
    qoffs = jnp.arange(A_BLOCK)
    koffs = jnp.arange(3 * A_BLOCK) - A_BLOCK
    band = jnp.abs(koffs[None, :] - qoffs[:, None]) <= A_WINDOW

    def block(i):
        start = i * A_BLOCK
        qb = lax.dynamic_slice_in_dim(ql, start, A_BLOCK, axis=1)
        kb = lax.dynamic_slice_in_dim(kp, start, 3 * A_BLOCK, axis=1)
        vb = lax.dynamic_slice_in_dim(vp, start, 3 * A_BLOCK, axis=1)
        kpos = start + koffs
        valid = band & ((kpos >= 0) & (kpos < S))[None, :]
        s_loc = jnp.einsum('bqhgd,bkhd->bhgqk', qb, kb, preferred_element_type=F32) * scale
        s_loc = jnp.where(valid, s_loc, NEG_INF)
        s_ctx = jnp.einsum('bqhgd,bchd->bhgqc', qb, kc, preferred_element_type=F32) * scale
        s_sink = jnp.broadcast_to(sink_g[None, :, :, None, None], (B, A_KV_HEADS, G, A_BLOCK, 1))
        p = jax.nn.softmax(jnp.concatenate([s_sink, s_ctx, s_loc], axis=-1), axis=-1).astype(vl.dtype)
        return (jnp.einsum('bhgqc,bchd->bqhgd', p[..., 1:1 + Lc], vc)
                + jnp.einsum('bhgqk,bkhd->bqhgd', p[..., 1 + Lc:], vb))

    ol = lax.map(block, jnp.arange(n_blocks))
    yl = jnp.moveaxis(ol, 0, 1).reshape(B, S, qw) @ wo
    if not need_ctx:
        return yl, None
    qc = q_proj(hc).reshape(B, Lc, A_KV_HEADS, G, A_HEAD_DIM)
    s = jnp.einsum('bqhgd,bchd->bhgqc', qc, kc, preferred_element_type=F32) * scale
    s_sink = jnp.broadcast_to(sink_g[None, :, :, None, None], (B, A_KV_HEADS, G, Lc, 1))
    p = jax.nn.softmax(jnp.concatenate([s_sink, s], axis=-1), axis=-1).astype(vc.dtype)
    yc = jnp.einsum('bhgqc,bchd->bqhgd', p[..., 1:], vc).reshape(B, Lc, qw) @ wo
    return yl, yc


def mlstm_scan(q, k, v, ig, lf, state):
    B, H, L, _ = q.shape
    dv = v.shape[-1]
    nc = L // M_CHUNK

    def chunks(t):
        return jnp.moveaxis(t.reshape((B, H, nc, M_CHUNK) + t.shape[3:]), 2, 0)

    tril = jnp.tril(jnp.ones((M_CHUNK, M_CHUNK), dtype=bool))

    def step(carry, xs):
        C, n, m = carry
        qc, kc, vc, ic, fc = xs
        qf = qc.astype(F32)
        kf = kc.astype(F32)
        vf = vc.astype(F32)
        b = jnp.cumsum(fc, axis=-1)
        log_d = jnp.where(tril, b[..., :, None] - b[..., None, :] + ic[..., None, :], NEG_INF)
        m_inter = b + m[..., None]
        m_t = jnp.maximum(jnp.max(log_d, axis=-1), m_inter)
        d_mat = jnp.exp(log_d - m_t[..., None])
        inter = jnp.exp(m_inter - m_t)
        s = jnp.einsum('bhtd,bhsd->bhts', qf, kf) * d_mat
        num = jnp.einsum('bhts,bhsv->bhtv', s, vf) + inter[..., None] * jnp.einsum('bhvd,bhtd->bhtv', C, qf)
        den = jnp.sum(s, axis=-1) + inter * jnp.einsum('bhd,bhtd->bht', n, qf)
        h = num / jnp.maximum(jnp.abs(den), jnp.exp(-m_t))[..., None]
        b_last = b[..., -1]
        log_w = b_last[..., None] - b + ic
        m_new = jnp.maximum(b_last + m, jnp.max(log_w, axis=-1))
        w = jnp.exp(log_w - m_new[..., None])
        decay = jnp.exp(b_last + m - m_new)
        C_new = decay[..., None, None] * C + jnp.einsum('bhs,bhsv,bhsd->bhvd', w, vf, kf)
        n_new = decay[..., None] * n + jnp.einsum('bhs,bhsd->bhd', w, kf)
        return (C_new, n_new, m_new), h

    state, hs = lax.scan(step, state, (chunks(q), chunks(k), chunks(v), chunks(ig), chunks(lf)))
    return jnp.moveaxis(hs, 0, 2).reshape(B, H, L, dv), state


def mlstm_mixer(hl, hc, w_in, b_gate, hnorm_g, wo, need_ctx):
    B = hl.shape[0]
    qk = M_HEADS * M_QK_DIM
    vw = M_HEADS * M_V_DIM

    def proj(h):
        L = h.shape[1]
        p = h @ w_in

        def heads(t, d):
            return t.reshape(B, L, M_HEADS, d).transpose(0, 2, 1, 3)

        q = heads(p[..., :qk], M_QK_DIM)
        k = heads(p[..., qk:2 * qk], M_QK_DIM) * (M_QK_DIM ** -0.5)
        v = heads(p[..., 2 * qk:2 * qk + vw], M_V_DIM)
        o = p[..., 2 * qk + vw:2 * qk + 2 * vw]
        g = (p[..., 2 * qk + 2 * vw:].astype(F32) + b_gate.astype(F32))
        g = g.reshape(B, L, 4, M_HEADS).transpose(2, 0, 3, 1)
        return q, k, v, o, g

    def flip(t):
        return jnp.flip(t, axis=2)

    def run(q, k, v, g, st_f, st_b):
        hf, sf = mlstm_scan(q, k, v, g[0], jax.nn.log_sigmoid(g[1]), st_f)
        hb, sb = mlstm_scan(flip(q), flip(k), flip(v), flip(g[2]), jax.nn.log_sigmoid(flip(g[3])), st_b)
        return hf + flip(hb), sf, sb

    def readout(h, o):
        L = h.shape[2]
        h = h.transpose(0, 2, 1, 3)
        h = h * lax.rsqrt(jnp.mean(h * h, axis=-1, keepdims=True) + NORM_EPS)
        h = h.reshape(B, L, vw) * hnorm_g.astype(F32) * jax.nn.sigmoid(o.astype(F32))
        return h.astype(o.dtype) @ wo

    init = (jnp.zeros((B, M_HEADS, M_V_DIM, M_QK_DIM), F32),
            jnp.zeros((B, M_HEADS, M_QK_DIM), F32),
            jnp.full((B, M_HEADS), NEG_INF, F32))
    qc, kc, vc, oc, gc = proj(hc)
    h_ctx, st_f, st_b = run(qc, kc, vc, gc, init, init)
    ql, kl, vl, ol, gl = proj(hl)
    h_lat, _, _ = run(ql, kl, vl, gl, st_f, st_b)
    yl = readout(h_lat, ol)
    yc = readout(h_ctx, oc) if need_ctx else None
    return yl, yc


def neighborhood_attn(hl, hc, wqkv, wo, rpb, need_ctx):
    B, S, _ = hl.shape
    Lc = hc.shape[1]
    R = S // GRID_W
    WR = min(NA_WIN_R, R)
    hw = C_HEADS * C_HEAD_DIM
    scale = C_HEAD_DIM ** -0.5
    p = hl @ wqkv
    q = p[..., :hw].reshape(B, R, GRID_W, C_HEADS, C_HEAD_DIM)
    k = p[..., hw:2 * hw].reshape(B, R, GRID_W, C_HEADS, C_HEAD_DIM)
    v = p[..., 2 * hw:].reshape(B, R, GRID_W, C_HEADS, C_HEAD_DIM)
    pc = hc @ wqkv[:, hw:]
    kc = pc[..., :hw].reshape(B, Lc, C_HEADS, C_HEAD_DIM)
    vc = pc[..., hw:].reshape(B, Lc, C_HEADS, C_HEAD_DIM)
    n_keys = WR * GRID_W
    qcol = jnp.arange(GRID_W)
    kcol = jnp.tile(jnp.arange(GRID_W), WR)
    krow = jnp.repeat(jnp.arange(WR), GRID_W)
    cstart = jnp.clip(qcol - NA_WIN_C // 2, 0, GRID_W - NA_WIN_C)
    col_in = (kcol[None, :] >= cstart[:, None]) & (kcol[None, :] < cstart[:, None] + NA_WIN_C)
    dc_idx = jnp.clip(kcol[None, :] - qcol[:, None] + NA_WIN_C - 1, 0, 2 * NA_WIN_C - 2)
    rpb32 = rpb.astype(F32)

    def row(r):
        rs = jnp.clip(r - WR // 2, 0, R - WR)
        qr = lax.dynamic_index_in_dim(q, r, axis=1, keepdims=False)
        kb = lax.dynamic_slice_in_dim(k, rs, WR, axis=1).reshape(B, n_keys, C_HEADS, C_HEAD_DIM)
        vb = lax.dynamic_slice_in_dim(v, rs, WR, axis=1).reshape(B, n_keys, C_HEADS, C_HEAD_DIM)
        dr_idx = rs + krow - r + NA_WIN_R - 1
        bias = rpb32[:, dr_idx[None, :], dc_idx]
        s_loc = jnp.einsum('bqhd,bkhd->bhqk', qr, kb, preferred_element_type=F32) * scale + bias
        s_loc = jnp.where(col_in, s_loc, NEG_INF)
        s_ctx = jnp.einsum('bqhd,bchd->bhqc', qr, kc, preferred_element_type=F32) * scale
        pr = jax.nn.softmax(jnp.concatenate([s_ctx, s_loc], axis=-1), axis=-1).astype(v.dtype)
        return (jnp.einsum('bhqc,bchd->bqhd', pr[..., :Lc], vc)
                + jnp.einsum('bhqk,bkhd->bqhd', pr[..., Lc:], vb))

    ol = lax.map(row, jnp.arange(R))
    yl = jnp.moveaxis(ol, 0, 1).reshape(B, S, hw) @ wo
    if not need_ctx:
        return yl, None
    qc = (hc @ wqkv[:, :hw]).reshape(B, Lc, C_HEADS, C_HEAD_DIM)
    s = jnp.einsum('bqhd,bchd->bhqc', qc, kc, preferred_element_type=F32) * scale
    pc_ = jax.nn.softmax(s, axis=-1).astype(vc.dtype)
    yc = jnp.einsum('bhqc,bchd->bqhd', pc_, vc).reshape(B, Lc, hw) @ wo
    return yl, yc


def moe_ffn(xt, router_w, router_b, w_in, b_in, w_out, b_out):
    T, D = xt.shape
    logits = jnp.dot(xt, router_w, preferred_element_type=F32) + router_b.astype(F32)
    top_val, top_idx = lax.top_k(logits, TOP_K)
    gates = jax.nn.softmax(top_val, axis=-1)
    e_flat = top_idx.reshape(-1).astype(jnp.int32)
    tok_flat = jnp.repeat(jnp.arange(T, dtype=jnp.int32), TOP_K)
    g_flat = gates.reshape(-1)
    order = jnp.argsort(e_flat)
    e_sorted = e_flat[order]
    counts = jnp.bincount(e_flat, length=N_EXPERTS)
    starts = jnp.cumsum(counts) - counts
    padded = (counts + MOE_BLOCK - 1) // MOE_BLOCK * MOE_BLOCK
    pad_ends = jnp.cumsum(padded)
    pad_starts = pad_ends - padded
    n_assign = T * TOP_K
    dest = pad_starts[e_sorted] + jnp.arange(n_assign, dtype=jnp.int32) - starts[e_sorted]
    n_blocks = -(-n_assign // MOE_BLOCK) + N_EXPERTS
    n_rows = n_blocks * MOE_BLOCK
    src_tok = jnp.full((n_rows,), T, jnp.int32).at[dest].set(tok_flat[order])
    row_gate = jnp.zeros((n_rows,), F32).at[dest].set(g_flat[order])
    x_pad = jnp.concatenate([xt, jnp.zeros((1, D), xt.dtype)], axis=0)
    xb = x_pad[src_tok].reshape(n_blocks, MOE_BLOCK, D)
    block_expert = jnp.minimum(
        jnp.searchsorted(pad_ends, jnp.arange(n_blocks) * MOE_BLOCK, side='right'), N_EXPERTS - 1)

    def expert_block(args):
        xe, e = args
        hcat = xe @ w_in[e] + b_in[e]
        gate = jnp.minimum(hcat[:, ::2], SWIGLU_LIMIT)
        up = jnp.clip(hcat[:, 1::2], -SWIGLU_LIMIT, SWIGLU_LIMIT)
        glu = gate * jax.nn.sigmoid(SWIGLU_ALPHA * gate)
        return ((up + 1) * glu) @ w_out[e] + b_out[e]

    yb = lax.map(expert_block, (xb, block_expert)).reshape(n_rows, D)
    y = jax.ops.segment_sum(yb.astype(F32) * row_gate[:, None], src_tok, num_segments=T + 1)[:T]
    return y.astype(xt.dtype)


def setup_inputs(seed: int = 0) -> dict:
    key = jax.random.key(seed)
    ks = jax.random.split(key, 32)

    def nrm(k, shape, s):
        return jax.random.normal(k, shape, F32) * s

    D = D_MODEL
    a_qkv_w = (A_HEADS + 2 * A_KV_HEADS) * A_HEAD_DIM
    m_in_w = 2 * M_HEADS * M_QK_DIM + 2 * M_HEADS * M_V_DIM + 4 * M_HEADS
    c_qkv_w = 3 * C_HEADS * C_HEAD_DIM
    fg_base = jnp.linspace(3.0, 6.0, M_HEADS, dtype=F32)[None, :]
    m_bgate = jnp.concatenate([
        nrm(ks[10], (N_LAYERS_B, M_HEADS), 0.1),
        fg_base + nrm(ks[11], (N_LAYERS_B, M_HEADS), 0.1),
        nrm(ks[12], (N_LAYERS_B, M_HEADS), 0.1),
        fg_base + nrm(ks[13], (N_LAYERS_B, M_HEADS), 0.1)], axis=-1)
    return {
        'x': nrm(ks[0], (BATCH, SEQ, D), 1.0),
        'c': nrm(ks[1], (BATCH, D), 1.0),
        'ctx': nrm(ks[2], (BATCH, CTX_LEN, D), 1.0),
        'c_ctx': nrm(ks[3], (D,), 1.0),
        'ada_w': nrm(ks[4], (DEPTH, D, N_MOD * D), 0.5 * D ** -0.5),
        'ada_b': nrm(ks[5], (DEPTH, N_MOD * D), 0.02),
        'norm1_g': 1.0 + nrm(ks[6], (DEPTH, D), 0.02),
        'norm2_g': 1.0 + nrm(ks[7], (DEPTH, D), 0.02),
        'a_wqkv': nrm(ks[8], (N_LAYERS_A, D, a_qkv_w), D ** -0.5),
        'a_wo': nrm(ks[9], (N_LAYERS_A, A_HEADS * A_HEAD_DIM, D), (A_HEADS * A_HEAD_DIM) ** -0.5),
        'a_sink': nrm(ks[14], (N_LAYERS_A, A_HEADS), 0.5),
        'm_win': nrm(ks[15], (N_LAYERS_B, D, m_in_w), D ** -0.5),
        'm_bgate': m_bgate,
        'm_hnorm': 1.0 + nrm(ks[16], (N_LAYERS_B, M_HEADS * M_V_DIM), 0.02),
        'm_wo': nrm(ks[17], (N_LAYERS_B, M_HEADS * M_V_DIM, D), (M_HEADS * M_V_DIM) ** -0.5),
        'n_wqkv': nrm(ks[18], (N_LAYERS_C, D, c_qkv_w), D ** -0.5),
        'n_wo': nrm(ks[19], (N_LAYERS_C, C_HEADS * C_HEAD_DIM, D), (C_HEADS * C_HEAD_DIM) ** -0.5),
        'n_rpb': nrm(ks[20], (N_LAYERS_C, C_HEADS, 2 * NA_WIN_R - 1, 2 * NA_WIN_C - 1), 0.1),
        'router_w': nrm(ks[21], (DEPTH, D, N_EXPERTS), D ** -0.5),
        'router_b': nrm(ks[22], (DEPTH, N_EXPERTS), 0.01),
        'exp_w_in': nrm(ks[23], (DEPTH, N_EXPERTS, D, 2 * D_FF), D ** -0.5),
        'exp_b_in': nrm(ks[24], (DEPTH, N_EXPERTS, 2 * D_FF), 0.01),
        'exp_w_out': nrm(ks[25], (DEPTH, N_EXPERTS, D_FF, D), D_FF ** -0.5),
        'exp_b_out': nrm(ks[26], (DEPTH, N_EXPERTS, D), 0.01),
        'final_g': 1.0 + nrm(ks[27], (D,), 0.02),
    }


def reference(x, c, ctx, c_ctx, ada_w, ada_b, norm1_g, norm2_g, a_wqkv, a_wo, a_sink,
              m_win, m_bgate, m_hnorm, m_wo, n_wqkv, n_wo, n_rpb,
              router_w, router_b, exp_w_in, exp_b_in, exp_w_out, exp_b_out, final_g):
    B, S, D = x.shape
    Lc = ctx.shape[1]
    pos = jnp.arange(S, dtype=jnp.int32)
    rows = pos // GRID_W
    cols = pos % GRID_W
    s_c = jax.nn.silu(c)
    s_cc = jax.nn.silu(c_ctx)
    for layer in range(DEPTH):
        last = layer == DEPTH - 1
        kind = layer % N_MIXERS
        j = layer // N_MIXERS
        m_l = (s_c @ ada_w[layer] + ada_b[layer]).reshape(B, 1, N_MOD, D)
        m_c = (s_cc @ ada_w[layer] + ada_b[layer]).reshape(1, 1, N_MOD, D)
        hl = modulate(rms_norm(x, norm1_g[layer]), m_l[:, :, 0], m_l[:, :, 1])
        hc = modulate(rms_norm(ctx, norm1_g[layer]), m_c[:, :, 0], m_c[:, :, 1])
        if kind == 0:
            yl, yc = window_gqa(hl, hc, a_wqkv[j], a_wo[j], a_sink[j], rows, cols, not last)
        elif kind == 1:
            yl, yc = mlstm_mixer(hl, hc, m_win[j], m_bgate[j], m_hnorm[j], m_wo[j], not last)
        else:
            yl, yc = neighborhood_attn(hl, hc, n_wqkv[j], n_wo[j], n_rpb[j], not last)
        x = x + m_l[:, :, 2] * yl
        h2l = modulate(rms_norm(x, norm2_g[layer]), m_l[:, :, 3], m_l[:, :, 4])
        if last:
            y = moe_ffn(h2l.reshape(B * S, D), router_w[layer], router_b[layer],
                        exp_w_in[layer], exp_b_in[layer], exp_w_out[layer], exp_b_out[layer])
            x = x + m_l[:, :, 5] * y.reshape(B, S, D)
        else:
            ctx = ctx + m_c[:, :, 2] * yc
            h2c = modulate(rms_norm(ctx, norm2_g[layer]), m_c[:, :, 3], m_c[:, :, 4])
            tokens = jnp.concatenate([h2l.reshape(B * S, D), h2c.reshape(B * Lc, D)], axis=0)
            y = moe_ffn(tokens, router_w[layer], router_b[layer],
                        exp_w_in[layer], exp_b_in[layer], exp_w_out[layer], exp_b_out[layer])
            x = x + m_l[:, :, 5] * y[:B * S].reshape(B, S, D)
            ctx = ctx + m_c[:, :, 5] * y[B * S:].reshape(B, Lc, D)
    return rms_norm(x, final_g)
```

```python
import functools

import jax
import jax.numpy as jnp
from jax import lax
from jax.experimental import pallas as pl
from jax.experimental.pallas import tpu as pltpu

F32 = jnp.float32
BF16 = jnp.bfloat16

D_MODEL = 1024
GRID_W = 64
N_MIXERS = 3
N_MOD = 6
NORM_EPS = 1e-6
NEG_INF = -1e30
A_HEADS = 16
A_KV_HEADS = 4
A_HEAD_DIM = D_MODEL // A_HEADS
A_WINDOW = 128
A_BLOCK = 128
ROPE_BASE = 10000.0
M_HEADS = 4
M_V_DIM = D_MODEL // M_HEADS
M_QK_DIM = M_V_DIM // 2
C_HEADS = 16
C_HEAD_DIM = D_MODEL // C_HEADS
NA_WIN_R = 8
NA_WIN_C = 16
N_EXPERTS = 32
TOP_K = 4
D_FF = D_MODEL
SWIGLU_LIMIT = 7.0
SWIGLU_ALPHA = 1.702
MOE_BLOCK = 512

LANES = 128
MLSTM_CHUNK = 128
NA_ROWS = 8
VMEM_LIMIT = 56 * 1024 * 1024


def _cparams(sem):
    return pltpu.CompilerParams(dimension_semantics=sem, vmem_limit_bytes=VMEM_LIMIT)


def _row_tile(S, n_ctx):
    for tm in (512, 256, 128):
        if S % tm == 0 and n_ctx % tm == 0:
            return tm
    raise ValueError("sequence lengths must be multiples of 128")


def _dot(a, b):
    return jnp.dot(a, b, preferred_element_type=F32)


def _dot_nt(a, b):
    return lax.dot_general(a, b, (((1,), (1,)), ((), ())), preferred_element_type=F32)


def _ada_body(c_ref, w_ref, b_ref, o_ref):
    c = c_ref[...]
    s = (c * jax.nn.sigmoid(c)).astype(BF16)
    o_ref[0] = _dot(s, w_ref[0].astype(BF16)) + b_ref[0]


def _ada_mods(c_all, ada_w, ada_b):
    depth, D, N = ada_w.shape
    R = c_all.shape[0]
    tn = 1536
    return pl.pallas_call(
        _ada_body,
        grid=(depth, N // tn),
        in_specs=[pl.BlockSpec((R, D), lambda l, j: (0, 0)),
                  pl.BlockSpec((1, D, tn), lambda l, j: (l, 0, j)),
                  pl.BlockSpec((1, 1, tn), lambda l, j: (l, 0, j))],
        out_specs=pl.BlockSpec((1, R, tn), lambda l, j: (l, 0, j)),
        out_shape=jax.ShapeDtypeStruct((depth, R, N), F32),
        compiler_params=_cparams(("arbitrary", "arbitrary")),
        name="ada_mods",
    )(c_all, ada_w, ada_b.reshape(depth, 1, N))


def _norm_mod(x, g, mod, shift_idx):
    ms = jnp.mean(x * x, axis=-1, keepdims=True)
    y = x * lax.rsqrt(ms + NORM_EPS) * g
    return y * (1.0 + mod[shift_idx + 1:shift_idx + 2, :]) + mod[shift_idx:shift_idx + 1, :]


def _proj_body(x_ref, mod_ref, g_ref, w_ref, *rest, rope_chunks):
    if rope_chunks:
        cos_ref, sin_ref, o_ref = rest
    else:
        (o_ref,) = rest
    h = _norm_mod(x_ref[...], g_ref[...], mod_ref[0], 0).astype(BF16)
    tm = h.shape[0]
    n_out = o_ref.shape[1]
    if rope_chunks:
        first = (lax.broadcasted_iota(jnp.int32, (tm, LANES), 1) % 32) < 16
        cos = cos_ref[...]
        sin = sin_ref[...]
    cw = 512
    for c0 in range(0, n_out, cw):
        c1 = min(c0 + cw, n_out)
        acc = _dot(h, w_ref[:, c0:c1])
        for l0 in range(c0, c1, LANES):
            t = acc[:, l0 - c0:l0 - c0 + LANES]
            if l0 // LANES < rope_chunks:
                partner = jnp.where(first, pltpu.roll(t, LANES - 16, 1), pltpu.roll(t, 16, 1))
                t = t * cos + partner * sin
            o_ref[:, l0:l0 + LANES] = t.astype(o_ref.dtype)


def _project(x, mods, g, w, geom, out_dtype, rope=None, rope_chunks=0):
    T, D = x.shape
    N = w.shape[1]
    tm, n_lat_tiles, tiles_per_batch, B = geom
    mod_idx = lambda i: jnp.where(i < n_lat_tiles, i // tiles_per_batch, B)
    in_specs = [pl.BlockSpec((tm, D), lambda i: (i, 0)),
                pl.BlockSpec((1, N_MOD, D), lambda i: (mod_idx(i), 0, 0)),
                pl.BlockSpec((1, D), lambda i: (0, 0)),
                pl.BlockSpec((D, N), lambda i: (0, 0))]
    args = [x, mods, g.reshape(1, D), w]
    if rope_chunks:
        in_specs += [pl.BlockSpec((tm, LANES), lambda i: (i, 0))] * 2
        args += list(rope)
    return pl.pallas_call(
        functools.partial(_proj_body, rope_chunks=rope_chunks),
        grid=(T // tm,),
        in_specs=in_specs,
        out_specs=pl.BlockSpec((tm, N), lambda i: (i, 0)),
        out_shape=jax.ShapeDtypeStruct((T, N), out_dtype),
        compiler_params=_cparams(("arbitrary",)),
        name="norm_mod_proj",
    )(*args)


def _rope_tables(B, S, n_ctx):
    quarter = A_HEAD_DIM // 4
    freqs = ROPE_BASE ** (-jnp.arange(quarter, dtype=F32) / quarter)
    pos = jnp.arange(S, dtype=jnp.int32)
    ang_r = (pos // GRID_W).astype(F32)[:, None] * freqs[None, :]
    ang_c = (pos % GRID_W).astype(F32)[:, None] * freqs[None, :]
    cos = jnp.concatenate([jnp.cos(ang_r)] * 2 + [jnp.cos(ang_c)] * 2, axis=-1)
    sin = jnp.concatenate([-jnp.sin(ang_r), jnp.sin(ang_r), -jnp.sin(ang_c), jnp.sin(ang_c)], axis=-1)
    cos = jnp.tile(jnp.tile(cos, (1, 2)), (B, 1))
    sin = jnp.tile(jnp.tile(sin, (1, 2)), (B, 1))
    cos = jnp.concatenate([cos, jnp.ones((n_ctx, LANES), F32)], axis=0)
    sin = jnp.concatenate([sin, jnp.zeros((n_ctx, LANES), F32)], axis=0)
    return cos, sin


def _wattn_body(sink_ref, q_ref, *rest, n_blocks, local):
    if local:
        kp_ref, kc_ref, kn_ref, vp_ref, vc_ref, vn_ref, kx_ref, vx_ref, o_ref = rest
    else:
        kx_ref, vx_ref, o_ref = rest
    G = A_HEADS // A_KV_HEADS
    d = A_HEAD_DIM
    blk = A_BLOCK
    scale = d ** -0.5
    n_ctx = kx_ref.shape[0]
    if local:
        i = pl.program_id(1)
        qpos = lax.broadcasted_iota(jnp.int32, (G * blk, 3 * blk), 0) % blk
        koff = lax.broadcasted_iota(jnp.int32, (G * blk, 3 * blk), 1) - blk
        band = jnp.abs(koff - qpos) <= A_WINDOW
        kpos = koff + i * blk
        valid = jnp.logical_and(band, jnp.logical_and(kpos >= 0, kpos < n_blocks * blk))
    outs = [None] * A_HEADS
    for h in range(A_KV_HEADS):
        cs = slice(h * d, (h + 1) * d)
        qh = jnp.concatenate([q_ref[:, (h * G + g) * d:(h * G + g + 1) * d] for g in range(G)], axis=0)
        sink = jnp.concatenate([jnp.full((blk, 1), sink_ref[h * G + g], F32) for g in range(G)], axis=0)
        s_ctx = _dot_nt(qh, kx_ref[:, cs]) * scale
        m = jnp.maximum(jnp.max(s_ctx, axis=-1, keepdims=True), sink)
        if local:
            k_loc = jnp.concatenate([kp_ref[:, cs], kc_ref[:, cs], kn_ref[:, cs]], axis=0)
            v_loc = jnp.concatenate([vp_ref[:, cs], vc_ref[:, cs], vn_ref[:, cs]], axis=0)
            s_loc = jnp.where(valid, _dot_nt(qh, k_loc) * scale, NEG_INF)
            m = jnp.maximum(m, jnp.max(s_loc, axis=-1, keepdims=True))
        p_ctx = jnp.exp(s_ctx - m)
        den = jnp.sum(p_ctx, axis=-1, keepdims=True) + jnp.exp(sink - m)
        o = _dot(p_ctx.astype(BF16), vx_ref[:, cs])
        if local:
            p_loc = jnp.exp(s_loc - m)
            den = den + jnp.sum(p_loc, axis=-1, keepdims=True)
            o = o + _dot(p_loc.astype(BF16), v_loc)
        o = o * (1.0 / den)
        for g in range(G):
            outs[h * G + g] = o[g * blk:(g + 1) * blk, :]
    o_ref[...] = jnp.concatenate(outs, axis=-1).astype(o_ref.dtype)


def _window_attention(qkv, sink, B, S, n_ctx_per):
    T = qkv.shape[0]
    blk = A_BLOCK
    qw = A_HEADS * A_HEAD_DIM
    kw = A_KV_HEADS * A_HEAD_DIM
    nb = S // blk
    kcol = qw // kw
    vcol = kcol + 1
    ctx_blk0 = (B * S) // n_ctx_per
    smem = pl.BlockSpec(memory_space=pltpu.SMEM)
    kx = pl.BlockSpec((n_ctx_per, kw), lambda b, i: (ctx_blk0 + b, kcol))
    vx = pl.BlockSpec((n_ctx_per, kw), lambda b, i: (ctx_blk0 + b, vcol))

    def nbr(col, delta):
        return pl.BlockSpec((blk, kw), lambda b, i: (b * nb + jnp.clip(i + delta, 0, nb - 1), col))

    lat = pl.pallas_call(
        functools.partial(_wattn_body, n_blocks=nb, local=True),
        grid=(B, nb),
        in_specs=[smem, pl.BlockSpec((blk, qw), lambda b, i: (b * nb + i, 0)),
                  nbr(kcol, -1), nbr(kcol, 0), nbr(kcol, 1),
                  nbr(vcol, -1), nbr(vcol, 0), nbr(vcol, 1), kx, vx],
        out_specs=pl.BlockSpec((blk, qw), lambda b, i: (b * nb + i, 0)),
        out_shape=jax.ShapeDtypeStruct((B * S, qw), BF16),
        compiler_params=_cparams(("arbitrary", "arbitrary")),
        name="window_attn",
    )(sink, qkv, qkv, qkv, qkv, qkv, qkv, qkv, qkv, qkv)
    ncb = n_ctx_per // blk
    q0 = (B * S) // blk
    ctx = pl.pallas_call(
        functools.partial(_wattn_body, n_blocks=ncb, local=False),
        grid=(B, ncb),
        in_specs=[smem, pl.BlockSpec((blk, qw), lambda b, i: (q0 + b * ncb + i, 0)), kx, vx],
        out_specs=pl.BlockSpec((blk, qw), lambda b, i: (b * ncb + i, 0)),
        out_shape=jax.ShapeDtypeStruct((B * n_ctx_per, qw), BF16),
        compiler_params=_cparams(("arbitrary", "arbitrary")),
        name="window_attn_ctx",
    )(sink, qkv, qkv, qkv)
    return jnp.concatenate([lat, ctx], axis=0)


def _nattn_body(q_ref, *rest, n_img_rows, local):
    if local:
        kp_ref, kc_ref, kn_ref, vp_ref, vc_ref, vn_ref, kx_ref, vx_ref, bias_ref, o_ref, kbuf, vbuf = rest
    else:
        kx_ref, vx_ref, o_ref = rest
    d = C_HEAD_DIM
    scale = d ** -0.5
    W = GRID_W
    n_keys = NA_WIN_R * W
    rows_per_step = q_ref.shape[0] // W

    def one_row(rr, off, dr0):
        qrow = q_ref[pl.ds(pl.multiple_of(rr * W, W), W), :]
        outs = []
        for h in range(C_HEADS):
            cs = slice(h * d, (h + 1) * d)
            qh = qrow[:, cs]
            s_ctx = _dot_nt(qh, kx_ref[:, cs]) * scale
            m = jnp.max(s_ctx, axis=-1, keepdims=True)
            if local:
                bias = jnp.concatenate(
                    [bias_ref[h, pl.ds(dr0 + 2 * t, 1)][0] for t in range(NA_WIN_R // 2)], axis=-1)
                s_loc = _dot_nt(qh, kbuf[pl.ds(off, n_keys), cs]) * scale + bias
                m = jnp.maximum(m, jnp.max(s_loc, axis=-1, keepdims=True))
            p_ctx = jnp.exp(s_ctx - m)
            den = jnp.sum(p_ctx, axis=-1, keepdims=True)
            o = _dot(p_ctx.astype(BF16), vx_ref[:, cs])
            if local:
                p_loc = jnp.exp(s_loc - m)
                den = den + jnp.sum(p_loc, axis=-1, keepdims=True)
                o = o + _dot(p_loc.astype(BF16), vbuf[pl.ds(off, n_keys), cs])
            outs.append(o * (1.0 / den))
        o_ref[pl.ds(pl.multiple_of(rr * W, W), W), :] = jnp.concatenate(outs, axis=-1).astype(o_ref.dtype)

    if local:
        j = pl.program_id(1)
        blk_tokens = rows_per_step * W
        for t, (kr, vr) in enumerate(((kp_ref, vp_ref), (kc_ref, vc_ref), (kn_ref, vn_ref))):
            kbuf[t * blk_tokens:(t + 1) * blk_tokens, :] = kr[...]
            vbuf[t * blk_tokens:(t + 1) * blk_tokens, :] = vr[...]

        def body(rr, carry):
            r = j * rows_per_step + rr
            rs = jnp.clip(r - NA_WIN_R // 2, 0, n_img_rows - NA_WIN_R)
            off = pl.multiple_of((rs - (j - 1) * rows_per_step) * W, W)
            one_row(rr, off, rs - r + NA_WIN_R - 1)
            return carry

        lax.fori_loop(0, rows_per_step, body, 0)
    else:
        def body(rr, carry):
            one_row(rr, None, None)
            return carry

        lax.fori_loop(0, rows_per_step, body, 0)


def _na_bias_table(rpb):
    W = GRID_W
    qcol = jnp.arange(W)
    kcol = jnp.arange(W)
    cstart = jnp.clip(qcol - NA_WIN_C // 2, 0, W - NA_WIN_C)
    col_in = (kcol[None, :] >= cstart[:, None]) & (kcol[None, :] < cstart[:, None] + NA_WIN_C)
    dc_idx = jnp.clip(kcol[None, :] - qcol[:, None] + NA_WIN_C - 1, 0, 2 * NA_WIN_C - 2)
    t2 = rpb.astype(F32)[:, :, dc_idx]
    t2 = jnp.where(col_in[None, None], t2, NEG_INF)
    return jnp.concatenate([t2[:, :-1], t2[:, 1:]], axis=-1)


def _neighborhood_attention(qkv, rpb, B, S, n_ctx_per):
    hw = C_HEADS * C_HEAD_DIM
    R = S // GRID_W
    rows = NA_ROWS
    blk = rows * GRID_W
    nj = R // rows
    ctx_blk0 = (B * S) // n_ctx_per
    kx = pl.BlockSpec((n_ctx_per, hw), lambda b, j: (ctx_blk0 + b, 1))
    vx = pl.BlockSpec((n_ctx_per, hw), lambda b, j: (ctx_blk0 + b, 2))
    bias = _na_bias_table(rpb)

    def nbr(col, delta):
        return pl.BlockSpec((blk, hw), lambda b, j: (b * nj + jnp.clip(j + delta, 0, nj - 1), col))

    lat = pl.pallas_call(
        functools.partial(_nattn_body, n_img_rows=R, local=True),
        grid=(B, nj),
        in_specs=[pl.BlockSpec((blk, hw), lambda b, j: (b * nj + j, 0)),
                  nbr(1, -1), nbr(1, 0), nbr(1, 1), nbr(2, -1), nbr(2, 0), nbr(2, 1), kx, vx,
                  pl.BlockSpec(bias.shape, lambda b, j: (0, 0, 0, 0))],
        out_specs=pl.BlockSpec((blk, hw), lambda b, j: (b * nj + j, 0)),
        out_shape=jax.ShapeDtypeStruct((B * S, hw), BF16),
        scratch_shapes=[pltpu.VMEM((3 * blk, hw), BF16), pltpu.VMEM((3 * blk, hw), BF16)],
        compiler_params=_cparams(("arbitrary", "arbitrary")),
        name="neighborhood_attn",
    )(qkv, qkv, qkv, qkv, qkv, qkv, qkv, qkv, qkv, bias)
    q0 = (B * S) // n_ctx_per
    ctx = pl.pallas_call(
        functools.partial(_nattn_body, n_img_rows=R, local=False),
        grid=(B, 1),
        in_specs=[pl.BlockSpec((n_ctx_per, hw), lambda b, j: (q0 + b, 0)), kx, vx],
        out_specs=pl.BlockSpec((n_ctx_per, hw), lambda b, j: (b, 0)),
        out_shape=jax.ShapeDtypeStruct((B * n_ctx_per, hw), BF16),
        compiler_params=_cparams(("arbitrary", "arbitrary")),
        name="neighborhood_attn_ctx",
    )(qkv, qkv, qkv)
    return jnp.concatenate([lat, ctx], axis=0)


def _log_sigmoid(x):
    return jnp.minimum(x, 0.0) - jnp.log1p(jnp.exp(-jnp.abs(x)))


def _mlstm_body(qf, kf, vf, gcf, grf, qb, kb, vb, gcb, grb, bgr_ref, bgc_ref, hf_ref, hb_ref, st_ref, m_ref):
    ch = qf.shape[0]
    dk, dv = M_QK_DIM, M_V_DIM
    H = M_HEADS

    @pl.when(pl.program_id(1) == 0)
    def _():
        st_ref[...] = jnp.zeros_like(st_ref)
        m_ref[...] = jnp.full_like(m_ref, NEG_INF)

    row_i = lax.broadcasted_iota(jnp.int32, (ch, ch), 0)
    col_i = lax.broadcasted_iota(jnp.int32, (ch, ch), 1)
    lower = row_i >= col_i
    upper = row_i <= col_i
    tril = lower.astype(F32)
    triu = upper.astype(F32)
    hi = lax.Precision.HIGHEST
    for direction, (q_ref, k_ref, v_ref, gc_ref, gr_ref, o_ref) in enumerate(
            ((qf, kf, vf, gcf, grf, hf_ref), (qb, kb, vb, gcb, grb, hb_ref))):
        fwd = direction == 0
        gcol = gc_ref[...] + bgr_ref[...]
        grow = gr_ref[...] + bgc_ref[...]
        mask = lower if fwd else upper
        cum_c = jnp.dot(tril if fwd else triu, _log_sigmoid(gcol), precision=hi, preferred_element_type=F32)
        cum_r = jnp.dot(_log_sigmoid(grow), triu if fwd else tril, precision=hi, preferred_element_type=F32)
        for h in range(H):
            c = direction * H + h
            gi = direction * 2 * H + h
            gf = gi + H
            b_col = cum_c[:, gf:gf + 1]
            b_row = cum_r[gf:gf + 1, :]
            i_col = gcol[:, gi:gi + 1]
            i_row = grow[gi:gi + 1, :]
            b_last = b_row[:, ch - 1:ch] if fwd else b_row[:, 0:1]
            m_prev = m_ref[c][:, 0:1]
            log_d = jnp.where(mask, b_col - b_row + i_row, NEG_INF)
            m_inter = b_col + m_prev
            m_t = jnp.maximum(jnp.max(log_d, axis=-1, keepdims=True), m_inter)
            d_mat = jnp.exp(log_d - m_t)
            inter = jnp.exp(m_inter - m_t)
            q = q_ref[:, h * dk:(h + 1) * dk].astype(BF16)
            k_s = k_ref[:, h * dk:(h + 1) * dk] * (dk ** -0.5)
            v = v_ref[:, h * dv:(h + 1) * dv]
            s = _dot_nt(q, k_s.astype(BF16)) * d_mat
            state = st_ref[c]
            q_state = _dot(q, state.astype(BF16))
            num = _dot(s.astype(BF16), v.astype(BF16)) + inter * q_state[:, :dv]
            den = jnp.sum(s, axis=-1, keepdims=True) + inter * q_state[:, dv:dv + 1]
            o_ref[:, h * dv:(h + 1) * dv] = num * (1.0 / jnp.maximum(jnp.abs(den), jnp.exp(-m_t)))
            m_new = jnp.maximum(b_last + m_prev, jnp.max(b_last - b_row + i_row, axis=-1, keepdims=True))
            w_col = jnp.exp(b_last - b_col + i_col - m_new)
            decay = jnp.exp(b_last + m_prev - m_new)
            wv = jnp.concatenate([v * w_col, jnp.broadcast_to(w_col, (ch, LANES))], axis=-1).astype(BF16)
            st_ref[c] = decay * state + _dot(k_s.T.astype(BF16), wv)
            m_ref[c] = jnp.broadcast_to(m_new, (1, LANES))


def _mlstm_scan(p, gates_t, b_gate, B, S, n_ctx_per):
    T = p.shape[0]
    ch = MLSTM_CHUNK
    H = M_HEADS
    vw = H * M_V_DIM
    ncc = n_ctx_per // ch
    nlc = S // ch
    gate_col = (2 * H * M_QK_DIM + 2 * vw) // LANES

    def fwd_idx(b, s):
        return jnp.where(s < ncc, (B * S + b * n_ctx_per) // ch + s, b * nlc + s - ncc)

    def bwd_idx(b, s):
        return jnp.where(s < ncc, (B * S + b * n_ctx_per) // ch + ncc - 1 - s, b * nlc + nlc - 1 - (s - ncc))

    def specs(idx):
        return [pl.BlockSpec((ch, H * M_QK_DIM), lambda b, s: (idx(b, s), 0)),
                pl.BlockSpec((ch, H * M_QK_DIM), lambda b, s: (idx(b, s), 1)),
                pl.BlockSpec((ch, vw), lambda b, s: (idx(b, s), 1)),
                pl.BlockSpec((ch, LANES), lambda b, s: (idx(b, s), gate_col)),
                pl.BlockSpec((4 * H, ch), lambda b, s: (0, idx(b, s)))]

    bg_row = jnp.zeros((1, LANES), F32).at[0, :4 * H].set(b_gate.astype(F32))
    bg_col = b_gate.astype(F32).reshape(4 * H, 1)
    return pl.pallas_call(
        _mlstm_body,
        grid=(B, ncc + nlc),
        in_specs=specs(fwd_idx) + specs(bwd_idx) + [pl.BlockSpec((1, LANES), lambda b, s: (0, 0)),
                                                     pl.BlockSpec((4 * H, 1), lambda b, s: (0, 0))],
        out_specs=[pl.BlockSpec((ch, vw), lambda b, s: (fwd_idx(b, s), 0)),
                   pl.BlockSpec((ch, vw), lambda b, s: (bwd_idx(b, s), 0))],
        out_shape=[jax.ShapeDtypeStruct((T, vw), F32)] * 2,
        scratch_shapes=[pltpu.VMEM((2 * H, M_QK_DIM, M_V_DIM + LANES), F32),
                        pltpu.VMEM((2 * H, 1, LANES), F32)],
        compiler_params=_cparams(("arbitrary", "arbitrary")),
        name="mlstm_scan",
    )(p, p, p, p, gates_t, p, p, p, p, gates_t, bg_row, bg_col)


def _outproj_body(*refs, mlstm):
    if mlstm:
        hf_ref, hb_ref, og_ref, hn_ref, x_ref, mod_ref, g2_ref, wo_ref, rw_ref, rb_ref, xo_ref, h2_ref, lg_ref = refs
        dv = M_V_DIM
        hsum = hf_ref[...] + hb_ref[...]
        parts = []
        for h in range(M_HEADS):
            t = hsum[:, h * dv:(h + 1) * dv]
            parts.append(t * lax.rsqrt(jnp.mean(t * t, axis=-1, keepdims=True) + NORM_EPS))
        a = (jnp.concatenate(parts, axis=-1) * hn_ref[...] * jax.nn.sigmoid(og_ref[...])).astype(BF16)
    else:
        a_ref, x_ref, mod_ref, g2_ref, wo_ref, rw_ref, rb_ref, xo_ref, h2_ref, lg_ref = refs
        a = a_ref[...]
    mod = mod_ref[0]
    x = x_ref[...] + mod[2:3, :] * _dot(a, wo_ref[...])
    xo_ref[...] = x
    h2 = _norm_mod(x, g2_ref[...], mod, 3)
    h2_ref[...] = h2.astype(BF16)
    lg_ref[...] = jnp.dot(h2, rw_ref[...], precision=lax.Precision.HIGHEST,
                          preferred_element_type=F32) + rb_ref[...]


def _out_project(mixer_out, x, mods, g2, wo, router_w, router_b, geom, mlstm):
    T, D = x.shape
    tm, n_lat_tiles, tiles_per_batch, B = geom
    mod_idx = lambda i: jnp.where(i < n_lat_tiles, i // tiles_per_batch, B)
    row = pl.BlockSpec((tm, D), lambda i: (i, 0))
    vec = pl.BlockSpec((1, D), lambda i: (0, 0))
    rw = jnp.zeros((D, LANES), F32).at[:, :N_EXPERTS].set(router_w.astype(F32))
    rb = jnp.zeros((1, LANES), F32).at[0, :N_EXPERTS].set(router_b.astype(F32))
    if mlstm:
        hf, hb, p, hnorm = mixer_out
        lead_specs = [row, row, pl.BlockSpec((tm, D), lambda i: (i, 2)), vec]
        lead = [hf, hb, p, hnorm.reshape(1, D)]
    else:
        lead_specs = [row]
        lead = [mixer_out]
    return pl.pallas_call(
        functools.partial(_outproj_body, mlstm=mlstm),
        grid=(T // tm,),
        in_specs=lead_specs + [row, pl.BlockSpec((1, N_MOD, D), lambda i: (mod_idx(i), 0, 0)), vec,
                               pl.BlockSpec((D, D), lambda i: (0, 0)),
                               pl.BlockSpec((D, LANES), lambda i: (0, 0)),
                               pl.BlockSpec((1, LANES), lambda i: (0, 0))],
        out_specs=[row, row, pl.BlockSpec((tm, LANES), lambda i: (i, 0))],
        out_shape=[jax.ShapeDtypeStruct((T, D), F32), jax.ShapeDtypeStruct((T, D), BF16),
                   jax.ShapeDtypeStruct((T, LANES), F32)],
        compiler_params=_cparams(("arbitrary",)),
        name="out_proj_norm_router",
    )(*lead, x, mods, g2.reshape(1, D), wo, rw, rb)


def _moe_body(be_ref, nused_ref, x_ref, rg_ref, wg_ref, wu_ref, wo_ref, bg_ref, bu_ref, bo_ref, o_ref):
    i = pl.program_id(0)

    @pl.when(i < nused_ref[0])
    def _():
        x = x_ref[...]
        gate = jnp.minimum(_dot(x, wg_ref[0]) + bg_ref[0], SWIGLU_LIMIT)
        up = jnp.clip(_dot(x, wu_ref[0]) + bu_ref[0], -SWIGLU_LIMIT, SWIGLU_LIMIT)
        act = (up + 1.0) * (gate * jax.nn.sigmoid(SWIGLU_ALPHA * gate))
        y = _dot(act.astype(BF16), wo_ref[0]) + bo_ref[0]
        o_ref[...] = y * rg_ref[...]

    @pl.when(i >= nused_ref[0])
    def _():
        o_ref[...] = jnp.zeros_like(o_ref)


def _moe_experts(xb, row_gate, block_expert, n_used, wg, wu, wo, bg, bu, bo):
    n_rows, D = xb.shape
    n_blocks = n_rows // MOE_BLOCK
    F = wg.shape[2]
    wspec = lambda shape: pl.BlockSpec((1,) + shape, lambda i, be, nu: (be[i], 0, 0))
    grid_spec = pltpu.PrefetchScalarGridSpec(
        num_scalar_prefetch=2,
        grid=(n_blocks,),
        in_specs=[pl.BlockSpec((MOE_BLOCK, D), lambda i, be, nu: (i, 0)),
                  pl.BlockSpec((MOE_BLOCK, 1), lambda i, be, nu: (i, 0)),
                  wspec((D, F)), wspec((D, F)), wspec((F, D)),
                  wspec((1, F)), wspec((1, F)), wspec((1, D))],
        out_specs=pl.BlockSpec((MOE_BLOCK, D), lambda i, be, nu: (i, 0)),
    )
    return pl.pallas_call(
        _moe_body,
        grid_spec=grid_spec,
        out_shape=jax.ShapeDtypeStruct((n_rows, D), F32),
        compiler_params=_cparams(("arbitrary",)),
        name="moe_experts",
    )(block_expert, n_used, xb, row_gate, wg, wu, wo, bg, bu, bo)


def _moe_layer(h2, logits, w_in, b_in, w_out, b_out):
    T, D = h2.shape
    top_val, top_idx = lax.top_k(logits[:, :N_EXPERTS], TOP_K)
    gates = jax.nn.softmax(top_val, axis=-1)
    onehot = jnp.sum(jax.nn.one_hot(top_idx, N_EXPERTS, dtype=jnp.int32), axis=1)
    before = jnp.cumsum(onehot, axis=0) - onehot
    counts = jnp.sum(onehot, axis=0)
    padded = (counts + MOE_BLOCK - 1) // MOE_BLOCK * MOE_BLOCK
    pad_ends = jnp.cumsum(padded)
    pad_starts = pad_ends - padded
    dest = pad_starts[top_idx] + jnp.take_along_axis(before, top_idx, axis=1)
    n_blocks = -(-(T * TOP_K) // MOE_BLOCK) + N_EXPERTS
    n_rows = n_blocks * MOE_BLOCK
    dest_flat = dest.reshape(-1)
    tok_flat = jnp.repeat(jnp.arange(T, dtype=jnp.int32), TOP_K)
    src_tok = jnp.full((n_rows,), T, jnp.int32).at[dest_flat].set(tok_flat, unique_indices=True)
    row_gate = jnp.zeros((n_rows,), F32).at[dest_flat].set(gates.reshape(-1), unique_indices=True)
    block_expert = jnp.minimum(
        jnp.searchsorted(pad_ends, jnp.arange(n_blocks) * MOE_BLOCK, side='right'), N_EXPERTS - 1).astype(jnp.int32)
    n_used = (pad_ends[-1] // MOE_BLOCK).astype(jnp.int32).reshape(1)
    x_pad = jnp.concatenate([h2, jnp.zeros((1, D), h2.dtype)], axis=0)
    xb = jnp.take(x_pad, src_tok, axis=0)
    wg = w_in[:, :, 0::2].astype(BF16)
    wu = w_in[:, :, 1::2].astype(BF16)
    bg = b_in[:, 0::2].astype(F32).reshape(N_EXPERTS, 1, -1)
    bu = b_in[:, 1::2].astype(F32).reshape(N_EXPERTS, 1, -1)
    yb = _moe_experts(xb, row_gate.reshape(n_rows, 1), block_expert, n_used, wg, wu, w_out.astype(BF16),
                      bg, bu, b_out.astype(F32).reshape(N_EXPERTS, 1, D))
    return jnp.sum(jnp.take(yb, dest, axis=0), axis=1)


def _final_norm_body(x_ref, g_ref, o_ref):
    x = x_ref[...]
    o_ref[...] = x * lax.rsqrt(jnp.mean(x * x, axis=-1, keepdims=True) + NORM_EPS) * g_ref[...]


def _final_norm(x, g, tm):
    T, D = x.shape
    return pl.pallas_call(
        _final_norm_body,
        grid=(T // tm,),
        in_specs=[pl.BlockSpec((tm, D), lambda i: (i, 0)), pl.BlockSpec((1, D), lambda i: (0, 0))],
        out_specs=pl.BlockSpec((tm, D), lambda i: (i, 0)),
        out_shape=jax.ShapeDtypeStruct((T, D), F32),
        compiler_params=_cparams(("arbitrary",)),
        name="final_norm",
    )(x, g.reshape(1, D))


def kernel(x, c, ctx, c_ctx, ada_w, ada_b, norm1_g, norm2_g, a_wqkv, a_wo, a_sink, m_win, m_bgate, m_hnorm,
           m_wo, n_wqkv, n_wo, n_rpb, router_w, router_b, exp_w_in, exp_b_in, exp_w_out, exp_b_out, final_g):
    B, S, D = x.shape
    Lc = ctx.shape[1]
    depth = ada_w.shape[0]
    n_lat, n_ctx = B * S, B * Lc
    assert D == D_MODEL and S % (NA_ROWS * GRID_W) == 0 and Lc % MLSTM_CHUNK == 0 and n_lat % Lc == 0
    tm = _row_tile(S, n_ctx)
    geom = (tm, n_lat // tm, S // tm, B)

    xs = jnp.concatenate([x.reshape(n_lat, D), ctx.reshape(n_ctx, D)], axis=0).astype(F32)
    c_rows = -(-(B + 1) // 8) * 8
    c_all = jnp.zeros((c_rows, D), F32).at[:B].set(c.astype(F32)).at[B].set(c_ctx.astype(F32))
    mods = _ada_mods(c_all, ada_w.astype(F32), ada_b.astype(F32)).reshape(depth, c_rows, N_MOD, D)
    rope = _rope_tables(B, S, n_ctx)

    for layer in range(depth):
        kind = layer % N_MIXERS
        j = layer // N_MIXERS
        mod_l = mods[layer]
        g1 = norm1_g[layer].astype(F32)
        if kind == 0:
            rope_cols = (A_HEADS + A_KV_HEADS) * A_HEAD_DIM
            qkv = _project(xs, mod_l, g1, a_wqkv[j].astype(BF16), geom, BF16, rope, rope_cols // LANES)
            mixed = _window_attention(qkv, a_sink[j].astype(F32), B, S, Lc)
            wo = a_wo[j]
        elif kind == 1:
            n_in = m_win.shape[2]
            n_pad = -(-n_in // LANES) * LANES
            w_in = jnp.zeros((D, n_pad), BF16).at[:, :n_in].set(m_win[j].astype(BF16))
            p = _project(xs, mod_l, g1, w_in, geom, F32)
            gates_t = p[:, n_in - 4 * M_HEADS:n_in].T
            hf, hb = _mlstm_scan(p, gates_t, m_bgate[j], B, S, Lc)
            mixed = (hf, hb, p, m_hnorm[j].astype(F32))
            wo = m_wo[j]
        else:
            qkv = _project(xs, mod_l, g1, n_wqkv[j].astype(BF16), geom, BF16)
            mixed = _neighborhood_attention(qkv, n_rpb[j], B, S, Lc)
            wo = n_wo[j]
        xs, h2, logits = _out_project(mixed, xs, mod_l, norm2_g[layer].astype(F32), wo.astype(BF16),
                                      router_w[layer], router_b[layer], geom, kind == 1)
        y = _moe_layer(h2, logits, exp_w_in[layer], exp_b_in[layer], exp_w_out[layer], exp_b_out[layer])
        gate2 = jnp.concatenate([jnp.repeat(mod_l[:B, 5], S, axis=0),
                                 jnp.broadcast_to(mod_l[B, 5], (n_ctx, D))], axis=0)
        xs = xs + gate2 * y
    return _final_norm(xs[:n_lat], final_g.astype(F32), tm).reshape(B, S, D).astype(x.dtype)
```

```python
import functools

import jax
import jax.numpy as jnp
from jax import lax
from jax.experimental import pallas as pl
from jax.experimental.pallas import tpu as pltpu

F32 = jnp.float32
BF16 = jnp.bfloat16

D_MODEL = 1024
GRID_W = 64
N_MIXERS = 3
N_MOD = 6
NORM_EPS = 1e-6
NEG_INF = -1e30
A_HEADS = 16
A_KV_HEADS = 4
A_HEAD_DIM = D_MODEL // A_HEADS
A_WINDOW = 128
A_BLOCK = 128
ROPE_BASE = 10000.0
M_HEADS = 4
M_V_DIM = D_MODEL // M_HEADS
M_QK_DIM = M_V_DIM // 2
C_HEADS = 16
C_HEAD_DIM = D_MODEL // C_HEADS
NA_WIN_R = 8
NA_WIN_C = 16
N_EXPERTS = 32
TOP_K = 4
D_FF = D_MODEL
SWIGLU_LIMIT = 7.0
SWIGLU_ALPHA = 1.702
MOE_BLOCK = 512

LANES = 128
MLSTM_CHUNK = 128
NA_ROWS = 8
VMEM_LIMIT = 56 * 1024 * 1024


def _cparams(sem):
    return pltpu.CompilerParams(dimension_semantics=sem, vmem_limit_bytes=VMEM_LIMIT)


def _row_tile(S, n_ctx):
    for tm in (512, 256, 128):
        if S % tm == 0 and n_ctx % tm == 0:
            return tm
    raise ValueError("sequence lengths must be multiples of 128")


def _dot(a, b):
    return jnp.dot(a, b, preferred_element_type=F32)


def _dot_nt(a, b):
    return lax.dot_general(a, b, (((1,), (1,)), ((), ())), preferred_element_type=F32)


def _ada_body(c_ref, w_ref, b_ref, o_ref):
    c = c_ref[...]
    s = (c * jax.nn.sigmoid(c)).astype(BF16)
    o_ref[0] = _dot(s, w_ref[0].astype(BF16)) + b_ref[0]


def _ada_mods(c_all, ada_w, ada_b):
    depth, D, N = ada_w.shape
    R = c_all.shape[0]
    tn = 1536
    return pl.pallas_call(
        _ada_body,
        grid=(depth, N // tn),
        in_specs=[pl.BlockSpec((R, D), lambda l, j: (0, 0)),
                  pl.BlockSpec((1, D, tn), lambda l, j: (l, 0, j)),
                  pl.BlockSpec((1, 1, tn), lambda l, j: (l, 0, j))],
        out_specs=pl.BlockSpec((1, R, tn), lambda l, j: (l, 0, j)),
        out_shape=jax.ShapeDtypeStruct((depth, R, N), F32),
        compiler_params=_cparams(("arbitrary", "arbitrary")),
        name="ada_mods",
    )(c_all, ada_w, ada_b.reshape(depth, 1, N))


def _norm_mod(x, g, mod, shift_idx):
    ms = jnp.mean(x * x, axis=-1, keepdims=True)
    y = x * lax.rsqrt(ms + NORM_EPS) * g
    return y * (1.0 + mod[shift_idx + 1:shift_idx + 2, :]) + mod[shift_idx:shift_idx + 1, :]


def _proj_body(x_ref, mod_ref, g_ref, w_ref, *rest, rope_chunks):
    if rope_chunks:
        cos_ref, sin_ref, o_ref = rest
    else:
        (o_ref,) = rest
    h = _norm_mod(x_ref[...], g_ref[...], mod_ref[0], 0).astype(BF16)
    tm = h.shape[0]
    n_out = o_ref.shape[1]
    if rope_chunks:
        first = (lax.broadcasted_iota(jnp.int32, (tm, LANES), 1) % 32) < 16
        cos = cos_ref[...]
        sin = sin_ref[...]
    cw = 512
    for c0 in range(0, n_out, cw):
        c1 = min(c0 + cw, n_out)
        acc = _dot(h, w_ref[:, c0:c1])
        for l0 in range(c0, c1, LANES):
            t = acc[:, l0 - c0:l0 - c0 + LANES]
            if l0 // LANES < rope_chunks:
                partner = jnp.where(first, pltpu.roll(t, LANES - 16, 1), pltpu.roll(t, 16, 1))
                t = t * cos + partner * sin
            o_ref[:, l0:l0 + LANES] = t.astype(o_ref.dtype)


def _project(x, mods, g, w, geom, out_dtype, rope=None, rope_chunks=0):
    T, D = x.shape
    N = w.shape[1]
    tm, n_lat_tiles, tiles_per_batch, B = geom
    mod_idx = lambda i: jnp.where(i < n_lat_tiles, i // tiles_per_batch, B)
    in_specs = [pl.BlockSpec((tm, D), lambda i: (i, 0)),
                pl.BlockSpec((1, N_MOD, D), lambda i: (mod_idx(i), 0, 0)),
                pl.BlockSpec((1, D), lambda i: (0, 0)),
                pl.BlockSpec((D, N), lambda i: (0, 0))]
    args = [x, mods, g.reshape(1, D), w]
    if rope_chunks:
        in_specs += [pl.BlockSpec((tm, LANES), lambda i: (i, 0))] * 2
        args += list(rope)
    return pl.pallas_call(
        functools.partial(_proj_body, rope_chunks=rope_chunks),
        grid=(T // tm,),
        in_specs=in_specs,
        out_specs=pl.BlockSpec((tm, N), lambda i: (i, 0)),
        out_shape=jax.ShapeDtypeStruct((T, N), out_dtype),
        compiler_params=_cparams(("arbitrary",)),
        name="norm_mod_proj",
    )(*args)


def _rope_tables(B, S, n_ctx):
    quarter = A_HEAD_DIM // 4
    freqs = ROPE_BASE ** (-jnp.arange(quarter, dtype=F32) / quarter)
    pos = jnp.arange(S, dtype=jnp.int32)
    ang_r = (pos // GRID_W).astype(F32)[:, None] * freqs[None, :]
    ang_c = (pos % GRID_W).astype(F32)[:, None] * freqs[None, :]
    cos = jnp.concatenate([jnp.cos(ang_r)] * 2 + [jnp.cos(ang_c)] * 2, axis=-1)
    sin = jnp.concatenate([-jnp.sin(ang_r), jnp.sin(ang_r), -jnp.sin(ang_c), jnp.sin(ang_c)], axis=-1)
    cos = jnp.tile(jnp.tile(cos, (1, 2)), (B, 1))
    sin = jnp.tile(jnp.tile(sin, (1, 2)), (B, 1))
    cos = jnp.concatenate([cos, jnp.ones((n_ctx, LANES), F32)], axis=0)
    sin = jnp.concatenate([sin, jnp.zeros((n_ctx, LANES), F32)], axis=0)
    return cos, sin


def _wattn_body(sink_ref, q_ref, *rest, n_blocks, local):
    if local:
        kp_ref, kc_ref, kn_ref, vp_ref, vc_ref, vn_ref, kx_ref, vx_ref, o_ref = rest
    else:
        kx_ref, vx_ref, o_ref = rest
    G = A_HEADS // A_KV_HEADS
    d = A_HEAD_DIM
    blk = A_BLOCK
    scale = d ** -0.5
    n_ctx = kx_ref.shape[0]
    if local:
        i = pl.program_id(1)
        qpos = lax.broadcasted_iota(jnp.int32, (G * blk, 3 * blk), 0) % blk
        koff = lax.broadcasted_iota(jnp.int32, (G * blk, 3 * blk), 1) - blk
        band = jnp.abs(koff - qpos) <= A_WINDOW
        kpos = koff + i * blk
        valid = jnp.logical_and(band, jnp.logical_and(kpos >= 0, kpos < n_blocks * blk))
    heads = [slice(h * d, (h + 1) * d) for h in range(A_KV_HEADS)]
    s_ctx, s_loc = [], []
    for h, cs in enumerate(heads):
        qh = jnp.concatenate([q_ref[:, (h * G + g) * d:(h * G + g + 1) * d] for g in range(G)], axis=0)
        s_ctx.append(_dot_nt(qh, kx_ref[:, cs]) * scale)
        if local:
            k_loc = jnp.concatenate([kp_ref[:, cs], kc_ref[:, cs], kn_ref[:, cs]], axis=0)
            s_loc.append(_dot_nt(qh, k_loc) * scale)
    p_ctx, p_loc, inv_den = [], [], []
    for h in range(A_KV_HEADS):
        sink = jnp.concatenate([jnp.full((blk, 1), sink_ref[h * G + g], F32) for g in range(G)], axis=0)
        m = jnp.maximum(jnp.max(s_ctx[h], axis=-1, keepdims=True), sink)
        if local:
            sl = jnp.where(valid, s_loc[h], NEG_INF)
            m = jnp.maximum(m, jnp.max(sl, axis=-1, keepdims=True))
        pc = jnp.exp(s_ctx[h] - m)
        den = jnp.sum(pc, axis=-1, keepdims=True) + jnp.exp(sink - m)
        p_ctx.append(pc.astype(BF16))
        if local:
            pl_ = jnp.exp(sl - m)
            den = den + jnp.sum(pl_, axis=-1, keepdims=True)
            p_loc.append(pl_.astype(BF16))
        inv_den.append(1.0 / den)
    outs = [None] * A_HEADS
    for h, cs in enumerate(heads):
        o = _dot(p_ctx[h], vx_ref[:, cs])
        if local:
            v_loc = jnp.concatenate([vp_ref[:, cs], vc_ref[:, cs], vn_ref[:, cs]], axis=0)
            o = o + _dot(p_loc[h], v_loc)
        o = o * inv_den[h]
        for g in range(G):
            outs[h * G + g] = o[g * blk:(g + 1) * blk, :]
    o_ref[...] = jnp.concatenate(outs, axis=-1).astype(o_ref.dtype)


def _window_attention(qkv, sink, B, S, n_ctx_per):
    T = qkv.shape[0]
    blk = A_BLOCK
    qw = A_HEADS * A_HEAD_DIM
    kw = A_KV_HEADS * A_HEAD_DIM
    nb = S // blk
    kcol = qw // kw
    vcol = kcol + 1
    ctx_blk0 = (B * S) // n_ctx_per
    smem = pl.BlockSpec(memory_space=pltpu.SMEM)
    kx = pl.BlockSpec((n_ctx_per, kw), lambda b, i: (ctx_blk0 + b, kcol))
    vx = pl.BlockSpec((n_ctx_per, kw), lambda b, i: (ctx_blk0 + b, vcol))

    def nbr(col, delta):
        return pl.BlockSpec((blk, kw), lambda b, i: (b * nb + jnp.clip(i + delta, 0, nb - 1), col))

    lat = pl.pallas_call(
        functools.partial(_wattn_body, n_blocks=nb, local=True),
        grid=(B, nb),
        in_specs=[smem, pl.BlockSpec((blk, qw), lambda b, i: (b * nb + i, 0)),
                  nbr(kcol, -1), nbr(kcol, 0), nbr(kcol, 1),
                  nbr(vcol, -1), nbr(vcol, 0), nbr(vcol, 1), kx, vx],
        out_specs=pl.BlockSpec((blk, qw), lambda b, i: (b * nb + i, 0)),
        out_shape=jax.ShapeDtypeStruct((B * S, qw), BF16),
        compiler_params=_cparams(("arbitrary", "arbitrary")),
        name="window_attn",
    )(sink, qkv, qkv, qkv, qkv, qkv, qkv, qkv, qkv, qkv)
    ncb = n_ctx_per // blk
    q0 = (B * S) // blk
    ctx = pl.pallas_call(
        functools.partial(_wattn_body, n_blocks=ncb, local=False),
        grid=(B, ncb),
        in_specs=[smem, pl.BlockSpec((blk, qw), lambda b, i: (q0 + b * ncb + i, 0)), kx, vx],
        out_specs=pl.BlockSpec((blk, qw), lambda b, i: (b * ncb + i, 0)),
        out_shape=jax.ShapeDtypeStruct((B * n_ctx_per, qw), BF16),
        compiler_params=_cparams(("arbitrary", "arbitrary")),
        name="window_attn_ctx",
    )(sink, qkv, qkv, qkv)
    return jnp.concatenate([lat, ctx], axis=0)


def _nattn_body(q_ref, *rest, n_img_rows, local):
    if local:
        kp_ref, kc_ref, kn_ref, vp_ref, vc_ref, vn_ref, kx_ref, vx_ref, bias_ref, o_ref, kbuf, vbuf = rest
    else:
        kx_ref, vx_ref, o_ref = rest
    d = C_HEAD_DIM
    scale = d ** -0.5
    W = GRID_W
    n_keys = NA_WIN_R * W
    rows_per_step = q_ref.shape[0] // W

    def one_row(rr, off, dr0):
        qrow = q_ref[pl.ds(pl.multiple_of(rr * W, W), W), :]
        heads = [slice(h * d, (h + 1) * d) for h in range(C_HEADS)]
        s_ctx = [_dot_nt(qrow[:, cs], kx_ref[:, cs]) * scale for cs in heads]
        if local:
            s_loc = [_dot_nt(qrow[:, cs], kbuf[pl.ds(off, n_keys), cs]) * scale for cs in heads]
        p_ctx, p_loc, inv_den = [], [], []
        for h in range(C_HEADS):
            m = jnp.max(s_ctx[h], axis=-1, keepdims=True)
            if local:
                bias = jnp.concatenate(
                    [bias_ref[h, pl.ds(dr0 + 2 * t, 1)][0] for t in range(NA_WIN_R // 2)], axis=-1)
                sl = s_loc[h] + bias
                m = jnp.maximum(m, jnp.max(sl, axis=-1, keepdims=True))
            pc = jnp.exp(s_ctx[h] - m)
            den = jnp.sum(pc, axis=-1, keepdims=True)
            p_ctx.append(pc.astype(BF16))
            if local:
                pl_ = jnp.exp(sl - m)
                den = den + jnp.sum(pl_, axis=-1, keepdims=True)
                p_loc.append(pl_.astype(BF16))
            inv_den.append(1.0 / den)
        outs = []
        for h, cs in enumerate(heads):
            o = _dot(p_ctx[h], vx_ref[:, cs])
            if local:
                o = o + _dot(p_loc[h], vbuf[pl.ds(off, n_keys), cs])
            outs.append(o * inv_den[h])
        o_ref[pl.ds(pl.multiple_of(rr * W, W), W), :] = jnp.concatenate(outs, axis=-1).astype(o_ref.dtype)

    if local:
        j = pl.program_id(1)
        blk_tokens = rows_per_step * W
        for t, (kr, vr) in enumerate(((kp_ref, vp_ref), (kc_ref, vc_ref), (kn_ref, vn_ref))):
            kbuf[t * blk_tokens:(t + 1) * blk_tokens, :] = kr[...]
            vbuf[t * blk_tokens:(t + 1) * blk_tokens, :] = vr[...]

        def body(rr, carry):
            r = j * rows_per_step + rr
            rs = jnp.clip(r - NA_WIN_R // 2, 0, n_img_rows - NA_WIN_R)
            off = pl.multiple_of((rs - (j - 1) * rows_per_step) * W, W)
            one_row(rr, off, rs - r + NA_WIN_R - 1)
            return carry

        lax.fori_loop(0, rows_per_step, body, 0)
    else:
        def body(rr, carry):
            one_row(rr, None, None)
            return carry

        lax.fori_loop(0, rows_per_step, body, 0)


def _na_bias_table(rpb):
    W = GRID_W
    qcol = jnp.arange(W)
    kcol = jnp.arange(W)
    cstart = jnp.clip(qcol - NA_WIN_C // 2, 0, W - NA_WIN_C)
    col_in = (kcol[None, :] >= cstart[:, None]) & (kcol[None, :] < cstart[:, None] + NA_WIN_C)
    dc_idx = jnp.clip(kcol[None, :] - qcol[:, None] + NA_WIN_C - 1, 0, 2 * NA_WIN_C - 2)
    t2 = rpb.astype(F32)[:, :, dc_idx]
    t2 = jnp.where(col_in[None, None], t2, NEG_INF)
    return jnp.concatenate([t2[:, :-1], t2[:, 1:]], axis=-1)


def _neighborhood_attention(qkv, rpb, B, S, n_ctx_per):
    hw = C_HEADS * C_HEAD_DIM
    R = S // GRID_W
    rows = NA_ROWS
    blk = rows * GRID_W
    nj = R // rows
    ctx_blk0 = (B * S) // n_ctx_per
    kx = pl.BlockSpec((n_ctx_per, hw), lambda b, j: (ctx_blk0 + b, 1))
    vx = pl.BlockSpec((n_ctx_per, hw), lambda b, j: (ctx_blk0 + b, 2))
    bias = _na_bias_table(rpb)

    def nbr(col, delta):
        return pl.BlockSpec((blk, hw), lambda b, j: (b * nj + jnp.clip(j + delta, 0, nj - 1), col))

    lat = pl.pallas_call(
        functools.partial(_nattn_body, n_img_rows=R, local=True),
        grid=(B, nj),
        in_specs=[pl.BlockSpec((blk, hw), lambda b, j: (b * nj + j, 0)),
                  nbr(1, -1), nbr(1, 0), nbr(1, 1), nbr(2, -1), nbr(2, 0), nbr(2, 1), kx, vx,
                  pl.BlockSpec(bias.shape, lambda b, j: (0, 0, 0, 0))],
        out_specs=pl.BlockSpec((blk, hw), lambda b, j: (b * nj + j, 0)),
        out_shape=jax.ShapeDtypeStruct((B * S, hw), BF16),
        scratch_shapes=[pltpu.VMEM((3 * blk, hw), BF16), pltpu.VMEM((3 * blk, hw), BF16)],
        compiler_params=_cparams(("arbitrary", "arbitrary")),
        name="neighborhood_attn",
    )(qkv, qkv, qkv, qkv, qkv, qkv, qkv, qkv, qkv, bias)
    q0 = (B * S) // n_ctx_per
    ctx = pl.pallas_call(
        functools.partial(_nattn_body, n_img_rows=R, local=False),
        grid=(B, 1),
        in_specs=[pl.BlockSpec((n_ctx_per, hw), lambda b, j: (q0 + b, 0)), kx, vx],
        out_specs=pl.BlockSpec((n_ctx_per, hw), lambda b, j: (b, 0)),
        out_shape=jax.ShapeDtypeStruct((B * n_ctx_per, hw), BF16),
        compiler_params=_cparams(("arbitrary", "arbitrary")),
        name="neighborhood_attn_ctx",
    )(qkv, qkv, qkv)
    return jnp.concatenate([lat, ctx], axis=0)


def _log_sigmoid(x):
    return jnp.minimum(x, 0.0) - jnp.log1p(jnp.exp(-jnp.abs(x)))


def _mlstm_body(qf, kf, vf, gcf, grf, qb, kb, vb, gcb, grb, bgr_ref, bgc_ref, hf_ref, hb_ref, st_ref, m_ref):
    ch = qf.shape[0]
    dk, dv = M_QK_DIM, M_V_DIM
    H = M_HEADS

    @pl.when(pl.program_id(1) == 0)
    def _():
        st_ref[...] = jnp.zeros_like(st_ref)
        m_ref[...] = jnp.full_like(m_ref, NEG_INF)

    row_i = lax.broadcasted_iota(jnp.int32, (ch, ch), 0)
    col_i = lax.broadcasted_iota(jnp.int32, (ch, ch), 1)
    lower = row_i >= col_i
    upper = row_i <= col_i
    tril = lower.astype(F32)
    triu = upper.astype(F32)
    hi = lax.Precision.HIGHEST
    dirs = ((qf, kf, vf, gcf, grf, hf_ref), (qb, kb, vb, gcb, grb, hb_ref))
    chains = []
    for direction, (q_ref, k_ref, v_ref, gc_ref, gr_ref, o_ref) in enumerate(dirs):
        fwd = direction == 0
        gcol = gc_ref[...] + bgr_ref[...]
        grow = gr_ref[...] + bgc_ref[...]
        cum_c = jnp.dot(tril if fwd else triu, _log_sigmoid(gcol), precision=hi, preferred_element_type=F32)
        cum_r = jnp.dot(_log_sigmoid(grow), triu if fwd else tril, precision=hi, preferred_element_type=F32)
        for h in range(H):
            c = direction * H + h
            q = q_ref[:, h * dk:(h + 1) * dk].astype(BF16)
            k_s = k_ref[:, h * dk:(h + 1) * dk] * (dk ** -0.5)
            state = st_ref[c]
            chains.append(dict(c=c, h=h, fwd=fwd, gcol=gcol, grow=grow, cum_c=cum_c, cum_r=cum_r, state=state,
                               k_t=k_s.T.astype(BF16), v=v_ref[:, h * dv:(h + 1) * dv], o_ref=o_ref,
                               s_raw=_dot_nt(q, k_s.astype(BF16)), q_state=_dot(q, state.astype(BF16))))
    for ck in chains:
        h, fwd = ck["h"], ck["fwd"]
        gi = (0 if fwd else 2 * H) + h
        gf = gi + H
        b_col = ck["cum_c"][:, gf:gf + 1]
        b_row = ck["cum_r"][gf:gf + 1, :]
        i_col = ck["gcol"][:, gi:gi + 1]
        i_row = ck["grow"][gi:gi + 1, :]
        b_last = b_row[:, ch - 1:ch] if fwd else b_row[:, 0:1]
        m_prev = m_ref[ck["c"]][:, 0:1]
        log_d = jnp.where(lower if fwd else upper, b_col - b_row + i_row, NEG_INF)
        m_inter = b_col + m_prev
        m_t = jnp.maximum(jnp.max(log_d, axis=-1, keepdims=True), m_inter)
        s = ck["s_raw"] * jnp.exp(log_d - m_t)
        m_new = jnp.maximum(b_last + m_prev, jnp.max(b_last - b_row + i_row, axis=-1, keepdims=True))
        w_col = jnp.exp(b_last - b_col + i_col - m_new)
        v = ck["v"]
        wv = jnp.concatenate([v * w_col, jnp.broadcast_to(w_col, (ch, LANES))], axis=-1).astype(BF16)
        ck.update(m_t=m_t, inter=jnp.exp(m_inter - m_t), m_new=m_new, decay=jnp.exp(b_last + m_prev - m_new),
                  s_sum=jnp.sum(s, axis=-1, keepdims=True), sv=_dot(s.astype(BF16), v.astype(BF16)),
                  upd=_dot(ck["k_t"], wv))
    for ck in chains:
        h, c = ck["h"], ck["c"]
        num = ck["sv"] + ck["inter"] * ck["q_state"][:, :dv]
        den = ck["s_sum"] + ck["inter"] * ck["q_state"][:, dv:dv + 1]
        ck["o_ref"][:, h * dv:(h + 1) * dv] = num * (1.0 / jnp.maximum(jnp.abs(den), jnp.exp(-ck["m_t"])))
        st_ref[c] = ck["decay"] * ck["state"] + ck["upd"]
        m_ref[c] = jnp.broadcast_to(ck["m_new"], (1, LANES))


def _mlstm_scan(p, gates_t, b_gate, B, S, n_ctx_per):
    T = p.shape[0]
    ch = MLSTM_CHUNK
    H = M_HEADS
    vw = H * M_V_DIM
    ncc = n_ctx_per // ch
    nlc = S // ch
    gate_col = (2 * H * M_QK_DIM + 2 * vw) // LANES

    def fwd_idx(b, s):
        return jnp.where(s < ncc, (B * S + b * n_ctx_per) // ch + s, b * nlc + s - ncc)

    def bwd_idx(b, s):
        return jnp.where(s < ncc, (B * S + b * n_ctx_per) // ch + ncc - 1 - s, b * nlc + nlc - 1 - (s - ncc))

    def specs(idx):
        return [pl.BlockSpec((ch, H * M_QK_DIM), lambda b, s: (idx(b, s), 0)),
                pl.BlockSpec((ch, H * M_QK_DIM), lambda b, s: (idx(b, s), 1)),
                pl.BlockSpec((ch, vw), lambda b, s: (idx(b, s), 1)),
                pl.BlockSpec((ch, LANES), lambda b, s: (idx(b, s), gate_col)),
                pl.BlockSpec((4 * H, ch), lambda b, s: (0, idx(b, s)))]

    bg_row = jnp.zeros((1, LANES), F32).at[0, :4 * H].set(b_gate.astype(F32))
    bg_col = b_gate.astype(F32).reshape(4 * H, 1)
    return pl.pallas_call(
        _mlstm_body,
        grid=(B, ncc + nlc),
        in_specs=specs(fwd_idx) + specs(bwd_idx) + [pl.BlockSpec((1, LANES), lambda b, s: (0, 0)),
                                                     pl.BlockSpec((4 * H, 1), lambda b, s: (0, 0))],
        out_specs=[pl.BlockSpec((ch, vw), lambda b, s: (fwd_idx(b, s), 0)),
                   pl.BlockSpec((ch, vw), lambda b, s: (bwd_idx(b, s), 0))],
        out_shape=[jax.ShapeDtypeStruct((T, vw), F32)] * 2,
        scratch_shapes=[pltpu.VMEM((2 * H, M_QK_DIM, M_V_DIM + LANES), F32),
                        pltpu.VMEM((2 * H, 1, LANES), F32)],
        compiler_params=_cparams(("arbitrary", "arbitrary")),
        name="mlstm_scan",
    )(p, p, p, p, gates_t, p, p, p, p, gates_t, bg_row, bg_col)


def _outproj_body(*refs, mlstm):
    if mlstm:
        hf_ref, hb_ref, og_ref, hn_ref, x_ref, mod_ref, g2_ref, wo_ref, rw_ref, rb_ref, xo_ref, h2_ref, lg_ref = refs
        dv = M_V_DIM
        hsum = hf_ref[...] + hb_ref[...]
        parts = []
        for h in range(M_HEADS):
            t = hsum[:, h * dv:(h + 1) * dv]
            parts.append(t * lax.rsqrt(jnp.mean(t * t, axis=-1, keepdims=True) + NORM_EPS))
        a = (jnp.concatenate(parts, axis=-1) * hn_ref[...] * jax.nn.sigmoid(og_ref[...])).astype(BF16)
    else:
        a_ref, x_ref, mod_ref, g2_ref, wo_ref, rw_ref, rb_ref, xo_ref, h2_ref, lg_ref = refs
        a = a_ref[...]
    mod = mod_ref[0]
    x = x_ref[...] + mod[2:3, :] * _dot(a, wo_ref[...])
    xo_ref[...] = x
    h2 = _norm_mod(x, g2_ref[...], mod, 3)
    h2_ref[...] = h2.astype(BF16)
    lg_ref[...] = jnp.dot(h2, rw_ref[...], precision=lax.Precision.HIGHEST,
                          preferred_element_type=F32) + rb_ref[...]


def _out_project(mixer_out, x, mods, g2, wo, router_w, router_b, geom, mlstm):
    T, D = x.shape
    tm, n_lat_tiles, tiles_per_batch, B = geom
    mod_idx = lambda i: jnp.where(i < n_lat_tiles, i // tiles_per_batch, B)
    row = pl.BlockSpec((tm, D), lambda i: (i, 0))
    vec = pl.BlockSpec((1, D), lambda i: (0, 0))
    rw = jnp.zeros((D, LANES), F32).at[:, :N_EXPERTS].set(router_w.astype(F32))
    rb = jnp.zeros((1, LANES), F32).at[0, :N_EXPERTS].set(router_b.astype(F32))
    if mlstm:
        hf, hb, p, hnorm = mixer_out
        lead_specs = [row, row, pl.BlockSpec((tm, D), lambda i: (i, 2)), vec]
        lead = [hf, hb, p, hnorm.reshape(1, D)]
    else:
        lead_specs = [row]
        lead = [mixer_out]
    return pl.pallas_call(
        functools.partial(_outproj_body, mlstm=mlstm),
        grid=(T // tm,),
        in_specs=lead_specs + [row, pl.BlockSpec((1, N_MOD, D), lambda i: (mod_idx(i), 0, 0)), vec,
                               pl.BlockSpec((D, D), lambda i: (0, 0)),
                               pl.BlockSpec((D, LANES), lambda i: (0, 0)),
                               pl.BlockSpec((1, LANES), lambda i: (0, 0))],
        out_specs=[row, row, pl.BlockSpec((tm, LANES), lambda i: (i, 0))],
        out_shape=[jax.ShapeDtypeStruct((T, D), F32), jax.ShapeDtypeStruct((T, D), BF16),
                   jax.ShapeDtypeStruct((T, LANES), F32)],
        compiler_params=_cparams(("arbitrary",)),
        name="out_proj_norm_router",
    )(*lead, x, mods, g2.reshape(1, D), wo, rw, rb)


def _moe_body(be_ref, nused_ref, x_ref, win_ref, wout_ref, bin_ref, bout_ref, o_ref, win_b, wout_b, stage, act_scr):
    i = pl.program_id(0)
    D, F2 = win_b.shape
    F = F2 // 2
    rows = x_ref.shape[0]
    used = i < nused_ref[0]
    changed = jnp.logical_or(i == 0, be_ref[i] != be_ref[jnp.maximum(i - 1, 0)])

    @pl.when(jnp.logical_and(used, changed))
    def _():
        def cast_rows(r, carry):
            rs = pl.ds(pl.multiple_of(r * LANES, LANES), LANES)
            win_b[rs, :] = win_ref[0, rs, :].astype(BF16)
            return carry

        lax.fori_loop(0, D // LANES, cast_rows, 0)
        for c in range(D // LANES):
            cs = slice(c * LANES, (c + 1) * LANES)
            stage[c, pl.ds(0, F // 2, stride=2), :] = wout_ref[0, :F // 2, cs]
            stage[c, pl.ds(1, F // 2, stride=2), :] = wout_ref[0, F // 2:, cs]
            wout_b[:, cs] = stage[c].astype(BF16)

    @pl.when(used)
    def _():
        x = x_ref[...]
        even = (lax.broadcasted_iota(jnp.int32, (rows, LANES), 1) % 2) == 0

        def glu_on_even_lanes(h):
            gate = jnp.minimum(h, SWIGLU_LIMIT)
            up = pltpu.roll(jnp.clip(h, -SWIGLU_LIMIT, SWIGLU_LIMIT), LANES - 1, 1)
            return (up + 1.0) * (gate * jax.nn.sigmoid(SWIGLU_ALPHA * gate))

        cw = 256
        for c0 in range(0, F, cw):
            ha = _dot(x, win_b[:, c0:c0 + cw]) + bin_ref[0, :, c0:c0 + cw]
            hb = _dot(x, win_b[:, F + c0:F + c0 + cw]) + bin_ref[0, :, F + c0:F + c0 + cw]
            for l0 in range(0, cw, LANES):
                a = glu_on_even_lanes(ha[:, l0:l0 + LANES])
                b = pltpu.roll(glu_on_even_lanes(hb[:, l0:l0 + LANES]), 1, 1)
                act_scr[:, c0 + l0:c0 + l0 + LANES] = jnp.where(even, a, b).astype(BF16)
        o_ref[...] = _dot(act_scr[...], wout_b[...]) + bout_ref[0]

    @pl.when(jnp.logical_not(used))
    def _():
        o_ref[...] = jnp.zeros_like(o_ref)


def _moe_experts(xb, block_expert, n_used, layer, w_in, b_in, w_out, b_out):
    n_rows, D = xb.shape
    n_blocks = n_rows // MOE_BLOCK
    L, E, _, F2 = w_in.shape
    F = F2 // 2
    wspec = lambda shape: pl.BlockSpec((1,) + shape, lambda i, be, nu: (layer * E + be[i], 0, 0))
    grid_spec = pltpu.PrefetchScalarGridSpec(
        num_scalar_prefetch=2,
        grid=(n_blocks,),
        in_specs=[pl.BlockSpec((MOE_BLOCK, D), lambda i, be, nu: (i, 0)),
                  wspec((D, F2)), wspec((F, D)), wspec((1, F2)), wspec((1, D))],
        out_specs=pl.BlockSpec((MOE_BLOCK, D), lambda i, be, nu: (i, 0)),
        scratch_shapes=[pltpu.VMEM((D, F2), BF16), pltpu.VMEM((F, D), BF16),
                        pltpu.VMEM((D // LANES, F, LANES), F32), pltpu.VMEM((MOE_BLOCK, F), BF16)],
    )
    return pl.pallas_call(
        _moe_body,
        grid_spec=grid_spec,
        out_shape=jax.ShapeDtypeStruct((n_rows, D), F32),
        compiler_params=_cparams(("arbitrary",)),
        name="moe_experts",
    )(block_expert, n_used, xb, w_in.reshape(L * E, D, F2), w_out.reshape(L * E, F, D),
      b_in.reshape(L * E, 1, F2), b_out.reshape(L * E, 1, D))


def _combine_body(g_ref, gate_ref, x_ref, mod_ref, o_ref):
    gates = gate_ref[...]
    y = gates[:, 0:1] * g_ref[0]
    for k in range(1, TOP_K):
        y = y + gates[:, k:k + 1] * g_ref[k]
    o_ref[...] = x_ref[...] + mod_ref[0][5:6, :] * y


def _moe_combine(gathered, gates, x, mods, geom):
    T, D = x.shape
    tm, n_lat_tiles, tiles_per_batch, B = geom
    mod_idx = lambda i: jnp.where(i < n_lat_tiles, i // tiles_per_batch, B)
    return pl.pallas_call(
        _combine_body,
        grid=(T // tm,),
        in_specs=[pl.BlockSpec((TOP_K, tm, D), lambda i: (0, i, 0)),
                  pl.BlockSpec((tm, TOP_K), lambda i: (i, 0)),
                  pl.BlockSpec((tm, D), lambda i: (i, 0)),
                  pl.BlockSpec((1, N_MOD, D), lambda i: (mod_idx(i), 0, 0))],
        out_specs=pl.BlockSpec((tm, D), lambda i: (i, 0)),
        out_shape=jax.ShapeDtypeStruct((T, D), F32),
        compiler_params=_cparams(("arbitrary",)),
        name="moe_combine",
    )(gathered, gates, x, mods)


def _moe_layer(xs, h2, logits, mods, geom, layer, w_in, b_in, w_out, b_out):
    T, D = h2.shape
    top_val, top_idx = lax.top_k(logits[:, :N_EXPERTS], TOP_K)
    gates = jax.nn.softmax(top_val, axis=-1)
    onehot = jnp.sum(jax.nn.one_hot(top_idx, N_EXPERTS, dtype=jnp.int32), axis=1)
    before = jnp.cumsum(onehot, axis=0) - onehot
    counts = jnp.sum(onehot, axis=0)
    starts = jnp.cumsum(counts) - counts
    padded = (counts + MOE_BLOCK - 1) // MOE_BLOCK * MOE_BLOCK
    pad_ends = jnp.cumsum(padded)
    pad_starts = pad_ends - padded
    dest = pad_starts[top_idx] + jnp.take_along_axis(before, top_idx, axis=1)
    n_assign = T * TOP_K
    n_blocks = -(-n_assign // MOE_BLOCK) + N_EXPERTS
    n_rows = n_blocks * MOE_BLOCK
    block_expert = jnp.minimum(
        jnp.searchsorted(pad_ends, jnp.arange(n_blocks) * MOE_BLOCK, side='right'), N_EXPERTS - 1).astype(jnp.int32)
    n_used = (pad_ends[-1] // MOE_BLOCK).astype(jnp.int32).reshape(1)
    order = jnp.argsort(top_idx.reshape(-1), stable=True).astype(jnp.int32)
    row_e = jnp.repeat(block_expert, MOE_BLOCK)
    off = jnp.arange(n_rows, dtype=jnp.int32) - pad_starts[row_e].astype(jnp.int32)
    src_sorted = jnp.clip(starts[row_e].astype(jnp.int32) + off, 0, n_assign - 1)
    src_tok = jnp.where(off < counts[row_e], order[src_sorted] // TOP_K, T)
    x_pad = jnp.concatenate([h2, jnp.zeros((8, D), h2.dtype)], axis=0)
    xb = x_pad.at[src_tok].get(mode='promise_in_bounds')
    yb = _moe_experts(xb, block_expert, n_used, layer, w_in, b_in, w_out, b_out)
    gathered = yb.at[dest.T.reshape(-1)].get(mode='promise_in_bounds').reshape(TOP_K, T, D)
    return _moe_combine(gathered, gates, xs, mods, geom)


def _final_norm_body(x_ref, g_ref, o_ref):
    x = x_ref[...]
    o_ref[...] = x * lax.rsqrt(jnp.mean(x * x, axis=-1, keepdims=True) + NORM_EPS) * g_ref[...]


def _final_norm(x, g, tm):
    T, D = x.shape
    return pl.pallas_call(
        _final_norm_body,
        grid=(T // tm,),
        in_specs=[pl.BlockSpec((tm, D), lambda i: (i, 0)), pl.BlockSpec((1, D), lambda i: (0, 0))],
        out_specs=pl.BlockSpec((tm, D), lambda i: (i, 0)),
        out_shape=jax.ShapeDtypeStruct((T, D), F32),
        compiler_params=_cparams(("arbitrary",)),
        name="final_norm",
    )(x, g.reshape(1, D))


def kernel(x, c, ctx, c_ctx, ada_w, ada_b, norm1_g, norm2_g, a_wqkv, a_wo, a_sink, m_win, m_bgate, m_hnorm,
           m_wo, n_wqkv, n_wo, n_rpb, router_w, router_b, exp_w_in, exp_b_in, exp_w_out, exp_b_out, final_g):
    B, S, D = x.shape
    Lc = ctx.shape[1]
    depth = ada_w.shape[0]
    n_lat, n_ctx = B * S, B * Lc
    assert D == D_MODEL and S % (NA_ROWS * GRID_W) == 0 and Lc % MLSTM_CHUNK == 0 and n_lat % Lc == 0
    tm = _row_tile(S, n_ctx)
    geom = (tm, n_lat // tm, S // tm, B)

    xs = jnp.concatenate([x.reshape(n_lat, D), ctx.reshape(n_ctx, D)], axis=0).astype(F32)
    c_rows = -(-(B + 1) // 8) * 8
    c_all = jnp.zeros((c_rows, D), F32).at[:B].set(c.astype(F32)).at[B].set(c_ctx.astype(F32))
    mods = _ada_mods(c_all, ada_w.astype(F32), ada_b.astype(F32)).reshape(depth, c_rows, N_MOD, D)
    rope = _rope_tables(B, S, n_ctx)

    for layer in range(depth):
        kind = layer % N_MIXERS
        j = layer // N_MIXERS
        mod_l = mods[layer]
        g1 = norm1_g[layer].astype(F32)
        if kind == 0:
            rope_cols = (A_HEADS + A_KV_HEADS) * A_HEAD_DIM
            qkv = _project(xs, mod_l, g1, a_wqkv[j].astype(BF16), geom, BF16, rope, rope_cols // LANES)
            mixed = _window_attention(qkv, a_sink[j].astype(F32), B, S, Lc)
            wo = a_wo[j]
        elif kind == 1:
            n_in = m_win.shape[2]
            n_pad = -(-n_in // LANES) * LANES
            w_in = jnp.zeros((D, n_pad), BF16).at[:, :n_in].set(m_win[j].astype(BF16))
            p = _project(xs, mod_l, g1, w_in, geom, F32)
            gates_t = p[:, n_in - 4 * M_HEADS:n_in].T
            hf, hb = _mlstm_scan(p, gates_t, m_bgate[j], B, S, Lc)
            mixed = (hf, hb, p, m_hnorm[j].astype(F32))
            wo = m_wo[j]
        else:
            qkv = _project(xs, mod_l, g1, n_wqkv[j].astype(BF16), geom, BF16)
            mixed = _neighborhood_attention(qkv, n_rpb[j], B, S, Lc)
            wo = n_wo[j]
        xs, h2, logits = _out_project(mixed, xs, mod_l, norm2_g[layer].astype(F32), wo.astype(BF16),
                                      router_w[layer], router_b[layer], geom, kind == 1)
        xs = _moe_layer(xs, h2, logits, mod_l, geom, layer, exp_w_in, exp_b_in, exp_w_out, exp_b_out)
    return _final_norm(xs[:n_lat], final_g.astype(F32), tm).reshape(B, S, D).astype(x.dtype)
```

```python
import functools

import jax
import jax.numpy as jnp
from jax import lax
from jax.experimental import pallas as pl
from jax.experimental.pallas import tpu as pltpu

F32 = jnp.float32
BF16 = jnp.bfloat16

D_MODEL = 1024
GRID_W = 64
N_MIXERS = 3
N_MOD = 6
NORM_EPS = 1e-6
NEG_INF = -1e30
A_HEADS = 16
A_KV_HEADS = 4
A_HEAD_DIM = D_MODEL // A_HEADS
A_WINDOW = 128
A_BLOCK = 128
ROPE_BASE = 10000.0
M_HEADS = 4
M_V_DIM = D_MODEL // M_HEADS
M_QK_DIM = M_V_DIM // 2
C_HEADS = 16
C_HEAD_DIM = D_MODEL // C_HEADS
NA_WIN_R = 8
NA_WIN_C = 16
N_EXPERTS = 32
TOP_K = 4
D_FF = D_MODEL
SWIGLU_LIMIT = 7.0
SWIGLU_ALPHA = 1.702
MOE_BLOCK = 512

LANES = 128
MLSTM_CHUNK = 128
NA_ROWS = 8
VMEM_LIMIT = 56 * 1024 * 1024


def _cparams(sem):
    return pltpu.CompilerParams(dimension_semantics=sem, vmem_limit_bytes=VMEM_LIMIT)


def _row_tile(S, n_ctx):
    for tm in (512, 256, 128):
        if S % tm == 0 and n_ctx % tm == 0:
            return tm
    raise ValueError("sequence lengths must be multiples of 128")


def _dot(a, b):
    return jnp.dot(a, b, preferred_element_type=F32)


def _dot_nt(a, b):
    return lax.dot_general(a, b, (((1,), (1,)), ((), ())), preferred_element_type=F32)


def _ada_body(c_ref, w_ref, b_ref, o_ref):
    c = c_ref[...]
    s = (c * jax.nn.sigmoid(c)).astype(BF16)
    o_ref[0] = _dot(s, w_ref[0].astype(BF16)) + b_ref[0]


def _ada_mods(c_all, ada_w, ada_b):
    depth, D, N = ada_w.shape
    R = c_all.shape[0]
    tn = 1536
    return pl.pallas_call(
        _ada_body,
        grid=(depth, N // tn),
        in_specs=[pl.BlockSpec((R, D), lambda l, j: (0, 0)),
                  pl.BlockSpec((1, D, tn), lambda l, j: (l, 0, j)),
                  pl.BlockSpec((1, 1, tn), lambda l, j: (l, 0, j))],
        out_specs=pl.BlockSpec((1, R, tn), lambda l, j: (l, 0, j)),
        out_shape=jax.ShapeDtypeStruct((depth, R, N), F32),
        compiler_params=_cparams(("arbitrary", "arbitrary")),
        name="ada_mods",
    )(c_all, ada_w, ada_b.reshape(depth, 1, N))


def _norm_mod(x, g, mod, shift_idx):
    ms = jnp.mean(x * x, axis=-1, keepdims=True)
    y = x * lax.rsqrt(ms + NORM_EPS) * g
    return y * (1.0 + mod[shift_idx + 1:shift_idx + 2, :]) + mod[shift_idx:shift_idx + 1, :]


def _proj_body(x_ref, mod_ref, g_ref, w_ref, *rest, rope_chunks):
    if rope_chunks:
        cos_ref, sin_ref, o_ref = rest
    else:
        (o_ref,) = rest
    h = _norm_mod(x_ref[...], g_ref[...], mod_ref[0], 0).astype(BF16)
    tm = h.shape[0]
    n_out = o_ref.shape[1]
    if rope_chunks:
        first = (lax.broadcasted_iota(jnp.int32, (tm, LANES), 1) % 32) < 16
        cos = cos_ref[...]
        sin = sin_ref[...]
    cw = 512
    for c0 in range(0, n_out, cw):
        c1 = min(c0 + cw, n_out)
        acc = _dot(h, w_ref[:, c0:c1])
        for l0 in range(c0, c1, LANES):
            t = acc[:, l0 - c0:l0 - c0 + LANES]
            if l0 // LANES < rope_chunks:
                partner = jnp.where(first, pltpu.roll(t, LANES - 16, 1), pltpu.roll(t, 16, 1))
                t = t * cos + partner * sin
            o_ref[:, l0:l0 + LANES] = t.astype(o_ref.dtype)


def _project(x, mods, g, w, geom, out_dtype, rope=None, rope_chunks=0):
    T, D = x.shape
    N = w.shape[1]
    tm, n_lat_tiles, tiles_per_batch, B = geom
    mod_idx = lambda i: jnp.where(i < n_lat_tiles, i // tiles_per_batch, B)
    in_specs = [pl.BlockSpec((tm, D), lambda i: (i, 0)),
                pl.BlockSpec((1, N_MOD, D), lambda i: (mod_idx(i), 0, 0)),
                pl.BlockSpec((1, D), lambda i: (0, 0)),
                pl.BlockSpec((D, N), lambda i: (0, 0))]
    args = [x, mods, g.reshape(1, D), w]
    if rope_chunks:
        in_specs += [pl.BlockSpec((tm, LANES), lambda i: (i, 0))] * 2
        args += list(rope)
    return pl.pallas_call(
        functools.partial(_proj_body, rope_chunks=rope_chunks),
        grid=(T // tm,),
        in_specs=in_specs,
        out_specs=pl.BlockSpec((tm, N), lambda i: (i, 0)),
        out_shape=jax.ShapeDtypeStruct((T, N), out_dtype),
        compiler_params=_cparams(("arbitrary",)),
        name="norm_mod_proj",
    )(*args)


def _rope_tables(B, S, n_ctx):
    quarter = A_HEAD_DIM // 4
    freqs = ROPE_BASE ** (-jnp.arange(quarter, dtype=F32) / quarter)
    pos = jnp.arange(S, dtype=jnp.int32)
    ang_r = (pos // GRID_W).astype(F32)[:, None] * freqs[None, :]
    ang_c = (pos % GRID_W).astype(F32)[:, None] * freqs[None, :]
    cos = jnp.concatenate([jnp.cos(ang_r)] * 2 + [jnp.cos(ang_c)] * 2, axis=-1)
    sin = jnp.concatenate([-jnp.sin(ang_r), jnp.sin(ang_r), -jnp.sin(ang_c), jnp.sin(ang_c)], axis=-1)
    cos = jnp.tile(jnp.tile(cos, (1, 2)), (B, 1))
    sin = jnp.tile(jnp.tile(sin, (1, 2)), (B, 1))
    cos = jnp.concatenate([cos, jnp.ones((n_ctx, LANES), F32)], axis=0)
    sin = jnp.concatenate([sin, jnp.zeros((n_ctx, LANES), F32)], axis=0)
    return cos, sin


def _wattn_body(sink_ref, q_ref, *rest, n_blocks, local):
    if local:
        kp_ref, kc_ref, kn_ref, vp_ref, vc_ref, vn_ref, kx_ref, vx_ref, o_ref = rest
    else:
        kx_ref, vx_ref, o_ref = rest
    G = A_HEADS // A_KV_HEADS
    d = A_HEAD_DIM
    blk = A_BLOCK
    scale = d ** -0.5
    n_ctx = kx_ref.shape[0]
    if local:
        i = pl.program_id(1)
        qpos = lax.broadcasted_iota(jnp.int32, (G * blk, 3 * blk), 0) % blk
        koff = lax.broadcasted_iota(jnp.int32, (G * blk, 3 * blk), 1) - blk
        band = jnp.abs(koff - qpos) <= A_WINDOW
        kpos = koff + i * blk
        valid = jnp.logical_and(band, jnp.logical_and(kpos >= 0, kpos < n_blocks * blk))
    heads = [slice(h * d, (h + 1) * d) for h in range(A_KV_HEADS)]
    s_ctx, s_loc = [], []
    for h, cs in enumerate(heads):
        qh = jnp.concatenate([q_ref[:, (h * G + g) * d:(h * G + g + 1) * d] for g in range(G)], axis=0)
        s_ctx.append(_dot_nt(qh, kx_ref[:, cs]) * scale)
        if local:
            k_loc = jnp.concatenate([kp_ref[:, cs], kc_ref[:, cs], kn_ref[:, cs]], axis=0)
            s_loc.append(_dot_nt(qh, k_loc) * scale)
    p_ctx, p_loc, inv_den = [], [], []
    for h in range(A_KV_HEADS):
        sink = jnp.concatenate([jnp.full((blk, 1), sink_ref[h * G + g], F32) for g in range(G)], axis=0)
        m = jnp.maximum(jnp.max(s_ctx[h], axis=-1, keepdims=True), sink)
        if local:
            sl = jnp.where(valid, s_loc[h], NEG_INF)
            m = jnp.maximum(m, jnp.max(sl, axis=-1, keepdims=True))
        pc = jnp.exp(s_ctx[h] - m)
        den = jnp.sum(pc, axis=-1, keepdims=True) + jnp.exp(sink - m)
        p_ctx.append(pc.astype(BF16))
        if local:
            pl_ = jnp.exp(sl - m)
            den = den + jnp.sum(pl_, axis=-1, keepdims=True)
            p_loc.append(pl_.astype(BF16))
        inv_den.append(1.0 / den)
    outs = [None] * A_HEADS
    for h, cs in enumerate(heads):
        o = _dot(p_ctx[h], vx_ref[:, cs])
        if local:
            v_loc = jnp.concatenate([vp_ref[:, cs], vc_ref[:, cs], vn_ref[:, cs]], axis=0)
            o = o + _dot(p_loc[h], v_loc)
        o = o * inv_den[h]
        for g in range(G):
            outs[h * G + g] = o[g * blk:(g + 1) * blk, :]
    o_ref[...] = jnp.concatenate(outs, axis=-1).astype(o_ref.dtype)


def _window_attention(qkv, sink, B, S, n_ctx_per):
    T = qkv.shape[0]
    blk = A_BLOCK
    qw = A_HEADS * A_HEAD_DIM
    kw = A_KV_HEADS * A_HEAD_DIM
    nb = S // blk
    kcol = qw // kw
    vcol = kcol + 1
    ctx_blk0 = (B * S) // n_ctx_per
    smem = pl.BlockSpec(memory_space=pltpu.SMEM)
    kx = pl.BlockSpec((n_ctx_per, kw), lambda b, i: (ctx_blk0 + b, kcol))
    vx = pl.BlockSpec((n_ctx_per, kw), lambda b, i: (ctx_blk0 + b, vcol))

    def nbr(col, delta):
        return pl.BlockSpec((blk, kw), lambda b, i: (b * nb + jnp.clip(i + delta, 0, nb - 1), col))

    lat = pl.pallas_call(
        functools.partial(_wattn_body, n_blocks=nb, local=True),
        grid=(B, nb),
        in_specs=[smem, pl.BlockSpec((blk, qw), lambda b, i: (b * nb + i, 0)),
                  nbr(kcol, -1), nbr(kcol, 0), nbr(kcol, 1),
                  nbr(vcol, -1), nbr(vcol, 0), nbr(vcol, 1), kx, vx],
        out_specs=pl.BlockSpec((blk, qw), lambda b, i: (b * nb + i, 0)),
        out_shape=jax.ShapeDtypeStruct((B * S, qw), BF16),
        compiler_params=_cparams(("arbitrary", "arbitrary")),
        name="window_attn",
    )(sink, qkv, qkv, qkv, qkv, qkv, qkv, qkv, qkv, qkv)
    ncb = n_ctx_per // blk
    q0 = (B * S) // blk
    ctx = pl.pallas_call(
        functools.partial(_wattn_body, n_blocks=ncb, local=False),
        grid=(B, ncb),
        in_specs=[smem, pl.BlockSpec((blk, qw), lambda b, i: (q0 + b * ncb + i, 0)), kx, vx],
        out_specs=pl.BlockSpec((blk, qw), lambda b, i: (b * ncb + i, 0)),
        out_shape=jax.ShapeDtypeStruct((B * n_ctx_per, qw), BF16),
        compiler_params=_cparams(("arbitrary", "arbitrary")),
        name="window_attn_ctx",
    )(sink, qkv, qkv, qkv)
    return jnp.concatenate([lat, ctx], axis=0)


def _nattn_body(q_ref, *rest, n_img_rows, local):
    if local:
        kp_ref, kc_ref, kn_ref, vp_ref, vc_ref, vn_ref, kx_ref, vx_ref, bias_ref, o_ref, kbuf, vbuf = rest
    else:
        kx_ref, vx_ref, o_ref = rest
    d = C_HEAD_DIM
    scale = d ** -0.5
    W = GRID_W
    n_keys = NA_WIN_R * W
    rows_per_step = q_ref.shape[0] // W

    def one_row(rr, off, dr0):
        qrow = q_ref[pl.ds(pl.multiple_of(rr * W, W), W), :]
        pairs = [slice(p * LANES, (p + 1) * LANES) for p in range(C_HEADS * d // LANES)]
        first = lax.broadcasted_iota(jnp.int32, (W, LANES), 1) < d
        zero = jnp.zeros((W, LANES), BF16)
        q_heads = []
        for ps in pairs:
            q_heads += [jnp.where(first, qrow[:, ps], zero), jnp.where(first, zero, qrow[:, ps])]
        s_ctx = [_dot_nt(q_heads[h], kx_ref[:, pairs[h // 2]]) * scale for h in range(C_HEADS)]
        if local:
            s_loc = [_dot_nt(q_heads[h], kbuf[pl.ds(off, n_keys), pairs[h // 2]]) * scale for h in range(C_HEADS)]
        p_ctx, p_loc, inv_den = [], [], []
        for h in range(C_HEADS):
            m = jnp.max(s_ctx[h], axis=-1, keepdims=True)
            if local:
                bias = jnp.concatenate(
                    [bias_ref[h, pl.ds(dr0 + 2 * t, 1)][0] for t in range(NA_WIN_R // 2)], axis=-1)
                sl = s_loc[h] + bias
                m = jnp.maximum(m, jnp.max(sl, axis=-1, keepdims=True))
            pc = jnp.exp(s_ctx[h] - m)
            den = jnp.sum(pc, axis=-1, keepdims=True)
            p_ctx.append(pc.astype(BF16))
            if local:
                pl_ = jnp.exp(sl - m)
                den = den + jnp.sum(pl_, axis=-1, keepdims=True)
                p_loc.append(pl_.astype(BF16))
            inv_den.append(1.0 / den)
        o_heads = []
        for h in range(C_HEADS):
            o = _dot(p_ctx[h], vx_ref[:, pairs[h // 2]])
            if local:
                o = o + _dot(p_loc[h], vbuf[pl.ds(off, n_keys), pairs[h // 2]])
            o_heads.append(o * inv_den[h])
        outs = [jnp.where(first, o_heads[2 * p], o_heads[2 * p + 1]) for p in range(len(pairs))]
        o_ref[pl.ds(pl.multiple_of(rr * W, W), W), :] = jnp.concatenate(outs, axis=-1).astype(o_ref.dtype)

    if local:
        j = pl.program_id(1)
        blk_tokens = rows_per_step * W
        for t, (kr, vr) in enumerate(((kp_ref, vp_ref), (kc_ref, vc_ref), (kn_ref, vn_ref))):
            kbuf[t * blk_tokens:(t + 1) * blk_tokens, :] = kr[...]
            vbuf[t * blk_tokens:(t + 1) * blk_tokens, :] = vr[...]

        def body(rr, carry):
            r = j * rows_per_step + rr
            rs = jnp.clip(r - NA_WIN_R // 2, 0, n_img_rows - NA_WIN_R)
            off = pl.multiple_of((rs - (j - 1) * rows_per_step) * W, W)
            one_row(rr, off, rs - r + NA_WIN_R - 1)
            return carry

        lax.fori_loop(0, rows_per_step, body, 0)
    else:
        def body(rr, carry):
            one_row(rr, None, None)
            return carry

        lax.fori_loop(0, rows_per_step, body, 0)


def _na_bias_table(rpb):
    W = GRID_W
    qcol = jnp.arange(W)
    kcol = jnp.arange(W)
    cstart = jnp.clip(qcol - NA_WIN_C // 2, 0, W - NA_WIN_C)
    col_in = (kcol[None, :] >= cstart[:, None]) & (kcol[None, :] < cstart[:, None] + NA_WIN_C)
    dc_idx = jnp.clip(kcol[None, :] - qcol[:, None] + NA_WIN_C - 1, 0, 2 * NA_WIN_C - 2)
    t2 = rpb.astype(F32)[:, :, dc_idx]
    t2 = jnp.where(col_in[None, None], t2, NEG_INF)
    return jnp.concatenate([t2[:, :-1], t2[:, 1:]], axis=-1)


def _neighborhood_attention(qkv, rpb, B, S, n_ctx_per):
    hw = C_HEADS * C_HEAD_DIM
    R = S // GRID_W
    rows = NA_ROWS
    blk = rows * GRID_W
    nj = R // rows
    ctx_blk0 = (B * S) // n_ctx_per
    kx = pl.BlockSpec((n_ctx_per, hw), lambda b, j: (ctx_blk0 + b, 1))
    vx = pl.BlockSpec((n_ctx_per, hw), lambda b, j: (ctx_blk0 + b, 2))
    bias = _na_bias_table(rpb)

    def nbr(col, delta):
        return pl.BlockSpec((blk, hw), lambda b, j: (b * nj + jnp.clip(j + delta, 0, nj - 1), col))

    lat = pl.pallas_call(
        functools.partial(_nattn_body, n_img_rows=R, local=True),
        grid=(B, nj),
        in_specs=[pl.BlockSpec((blk, hw), lambda b, j: (b * nj + j, 0)),
                  nbr(1, -1), nbr(1, 0), nbr(1, 1), nbr(2, -1), nbr(2, 0), nbr(2, 1), kx, vx,
                  pl.BlockSpec(bias.shape, lambda b, j: (0, 0, 0, 0))],
        out_specs=pl.BlockSpec((blk, hw), lambda b, j: (b * nj + j, 0)),
        out_shape=jax.ShapeDtypeStruct((B * S, hw), BF16),
        scratch_shapes=[pltpu.VMEM((3 * blk, hw), BF16), pltpu.VMEM((3 * blk, hw), BF16)],
        compiler_params=_cparams(("arbitrary", "arbitrary")),
        name="neighborhood_attn",
    )(qkv, qkv, qkv, qkv, qkv, qkv, qkv, qkv, qkv, bias)
    q0 = (B * S) // n_ctx_per
    ctx = pl.pallas_call(
        functools.partial(_nattn_body, n_img_rows=R, local=False),
        grid=(B, 1),
        in_specs=[pl.BlockSpec((n_ctx_per, hw), lambda b, j: (q0 + b, 0)), kx, vx],
        out_specs=pl.BlockSpec((n_ctx_per, hw), lambda b, j: (b, 0)),
        out_shape=jax.ShapeDtypeStruct((B * n_ctx_per, hw), BF16),
        compiler_params=_cparams(("arbitrary", "arbitrary")),
        name="neighborhood_attn_ctx",
    )(qkv, qkv, qkv)
    return jnp.concatenate([lat, ctx], axis=0)


def _log_sigmoid(x):
    return jnp.minimum(x, 0.0) - jnp.log1p(jnp.exp(-jnp.abs(x)))


def _mlstm_body(qf, kf, vf, gcf, grf, qb, kb, vb, gcb, grb, bgr_ref, bgc_ref, hf_ref, hb_ref, st_ref, m_ref):
    ch = qf.shape[0]
    dk, dv = M_QK_DIM, M_V_DIM
    H = M_HEADS

    @pl.when(pl.program_id(1) == 0)
    def _():
        st_ref[...] = jnp.zeros_like(st_ref)
        m_ref[...] = jnp.full_like(m_ref, NEG_INF)

    row_i = lax.broadcasted_iota(jnp.int32, (ch, ch), 0)
    col_i = lax.broadcasted_iota(jnp.int32, (ch, ch), 1)
    lower = row_i >= col_i
    upper = row_i <= col_i
    tril = lower.astype(F32)
    triu = upper.astype(F32)
    hi = lax.Precision.HIGHEST
    dirs = ((qf, kf, vf, gcf, grf, hf_ref), (qb, kb, vb, gcb, grb, hb_ref))
    chains = []
    for direction, (q_ref, k_ref, v_ref, gc_ref, gr_ref, o_ref) in enumerate(dirs):
        fwd = direction == 0
        gcol = gc_ref[...] + bgr_ref[...]
        grow = gr_ref[...] + bgc_ref[...]
        cum_c = jnp.dot(tril if fwd else triu, _log_sigmoid(gcol), precision=hi, preferred_element_type=F32)
        cum_r = jnp.dot(_log_sigmoid(grow), triu if fwd else tril, precision=hi, preferred_element_type=F32)
        for h in range(H):
            c = direction * H + h
            q = q_ref[:, h * dk:(h + 1) * dk].astype(BF16)
            k_s = k_ref[:, h * dk:(h + 1) * dk] * (dk ** -0.5)
            state = st_ref[c]
            chains.append(dict(c=c, h=h, fwd=fwd, gcol=gcol, grow=grow, cum_c=cum_c, cum_r=cum_r, state=state,
                               k_t=k_s.T.astype(BF16), v=v_ref[:, h * dv:(h + 1) * dv], o_ref=o_ref,
                               s_raw=_dot_nt(q, k_s.astype(BF16)), q_state=_dot(q, state.astype(BF16))))
    for ck in chains:
        h, fwd = ck["h"], ck["fwd"]
        gi = (0 if fwd else 2 * H) + h
        gf = gi + H
        b_col = ck["cum_c"][:, gf:gf + 1]
        b_row = ck["cum_r"][gf:gf + 1, :]
        i_col = ck["gcol"][:, gi:gi + 1]
        i_row = ck["grow"][gi:gi + 1, :]
        b_last = b_row[:, ch - 1:ch] if fwd else b_row[:, 0:1]
        m_prev = m_ref[ck["c"]][:, 0:1]
        log_d = jnp.where(lower if fwd else upper, b_col - b_row + i_row, NEG_INF)
        m_inter = b_col + m_prev
        m_t = jnp.maximum(jnp.max(log_d, axis=-1, keepdims=True), m_inter)
        s = ck["s_raw"] * jnp.exp(log_d - m_t)
        m_new = jnp.maximum(b_last + m_prev, jnp.max(b_last - b_row + i_row, axis=-1, keepdims=True))
        w_col = jnp.exp(b_last - b_col + i_col - m_new)
        v = ck["v"]
        wv = jnp.concatenate([v * w_col, jnp.broadcast_to(w_col, (ch, LANES))], axis=-1).astype(BF16)
        ck.update(m_t=m_t, inter=jnp.exp(m_inter - m_t), m_new=m_new, decay=jnp.exp(b_last + m_prev - m_new),
                  s_sum=jnp.sum(s, axis=-1, keepdims=True), sv=_dot(s.astype(BF16), v.astype(BF16)),
                  upd=_dot(ck["k_t"], wv))
    for ck in chains:
        h, c = ck["h"], ck["c"]
        num = ck["sv"] + ck["inter"] * ck["q_state"][:, :dv]
        den = ck["s_sum"] + ck["inter"] * ck["q_state"][:, dv:dv + 1]
        ck["o_ref"][:, h * dv:(h + 1) * dv] = num * (1.0 / jnp.maximum(jnp.abs(den), jnp.exp(-ck["m_t"])))
        st_ref[c] = ck["decay"] * ck["state"] + ck["upd"]
        m_ref[c] = jnp.broadcast_to(ck["m_new"], (1, LANES))


def _mlstm_scan(p, gates_t, b_gate, B, S, n_ctx_per):
    T = p.shape[0]
    ch = MLSTM_CHUNK
    H = M_HEADS
    vw = H * M_V_DIM
    ncc = n_ctx_per // ch
    nlc = S // ch
    gate_col = (2 * H * M_QK_DIM + 2 * vw) // LANES

    def fwd_idx(b, s):
        return jnp.where(s < ncc, (B * S + b * n_ctx_per) // ch + s, b * nlc + s - ncc)

    def bwd_idx(b, s):
        return jnp.where(s < ncc, (B * S + b * n_ctx_per) // ch + ncc - 1 - s, b * nlc + nlc - 1 - (s - ncc))

    def specs(idx):
        return [pl.BlockSpec((ch, H * M_QK_DIM), lambda b, s: (idx(b, s), 0)),
                pl.BlockSpec((ch, H * M_QK_DIM), lambda b, s: (idx(b, s), 1)),
                pl.BlockSpec((ch, vw), lambda b, s: (idx(b, s), 1)),
                pl.BlockSpec((ch, LANES), lambda b, s: (idx(b, s), gate_col)),
                pl.BlockSpec((4 * H, ch), lambda b, s: (0, idx(b, s)))]

    bg_row = jnp.zeros((1, LANES), F32).at[0, :4 * H].set(b_gate.astype(F32))
    bg_col = b_gate.astype(F32).reshape(4 * H, 1)
    return pl.pallas_call(
        _mlstm_body,
        grid=(B, ncc + nlc),
        in_specs=specs(fwd_idx) + specs(bwd_idx) + [pl.BlockSpec((1, LANES), lambda b, s: (0, 0)),
                                                     pl.BlockSpec((4 * H, 1), lambda b, s: (0, 0))],
        out_specs=[pl.BlockSpec((ch, vw), lambda b, s: (fwd_idx(b, s), 0)),
                   pl.BlockSpec((ch, vw), lambda b, s: (bwd_idx(b, s), 0))],
        out_shape=[jax.ShapeDtypeStruct((T, vw), F32)] * 2,
        scratch_shapes=[pltpu.VMEM((2 * H, M_QK_DIM, M_V_DIM + LANES), F32),
                        pltpu.VMEM((2 * H, 1, LANES), F32)],
        compiler_params=_cparams(("arbitrary", "arbitrary")),
        name="mlstm_scan",
    )(p, p, p, p, gates_t, p, p, p, p, gates_t, bg_row, bg_col)


def _outproj_body(*refs, mlstm):
    if mlstm:
        hf_ref, hb_ref, og_ref, hn_ref, x_ref, mod_ref, g2_ref, wo_ref, rw_ref, rb_ref, xo_ref, h2_ref, lg_ref = refs
        dv = M_V_DIM
        hsum = hf_ref[...] + hb_ref[...]
        parts = []
        for h in range(M_HEADS):
            t = hsum[:, h * dv:(h + 1) * dv]
            parts.append(t * lax.rsqrt(jnp.mean(t * t, axis=-1, keepdims=True) + NORM_EPS))
        a = (jnp.concatenate(parts, axis=-1) * hn_ref[...] * jax.nn.sigmoid(og_ref[...])).astype(BF16)
    else:
        a_ref, x_ref, mod_ref, g2_ref, wo_ref, rw_ref, rb_ref, xo_ref, h2_ref, lg_ref = refs
        a = a_ref[...]
    mod = mod_ref[0]
    x = x_ref[...] + mod[2:3, :] * _dot(a, wo_ref[...])
    xo_ref[...] = x
    h2 = _norm_mod(x, g2_ref[...], mod, 3)
    h2_hi = h2.astype(BF16)
    h2_ref[...] = h2_hi
    h2_lo = (h2 - h2_hi.astype(F32)).astype(BF16)
    t = _dot(h2_hi, rw_ref[...])
    lg_ref[...] = t[:, :LANES] + (t[:, LANES:] + _dot(h2_lo, rw_ref[:, :LANES])) + rb_ref[...]


def _out_project(mixer_out, x, mods, g2, wo, router_w, router_b, geom, mlstm):
    T, D = x.shape
    tm, n_lat_tiles, tiles_per_batch, B = geom
    mod_idx = lambda i: jnp.where(i < n_lat_tiles, i // tiles_per_batch, B)
    row = pl.BlockSpec((tm, D), lambda i: (i, 0))
    vec = pl.BlockSpec((1, D), lambda i: (0, 0))
    rw = jnp.zeros((D, LANES), F32).at[:, :N_EXPERTS].set(router_w.astype(F32))
    rw_hi = rw.astype(BF16)
    rw = jnp.concatenate([rw_hi, (rw - rw_hi.astype(F32)).astype(BF16)], axis=1)
    rb = jnp.zeros((1, LANES), F32).at[0, :N_EXPERTS].set(router_b.astype(F32))
    if mlstm:
        hf, hb, p, hnorm = mixer_out
        lead_specs = [row, row, pl.BlockSpec((tm, D), lambda i: (i, 2)), vec]
        lead = [hf, hb, p, hnorm.reshape(1, D)]
    else:
        lead_specs = [row]
        lead = [mixer_out]
    return pl.pallas_call(
        functools.partial(_outproj_body, mlstm=mlstm),
        grid=(T // tm,),
        in_specs=lead_specs + [row, pl.BlockSpec((1, N_MOD, D), lambda i: (mod_idx(i), 0, 0)), vec,
                               pl.BlockSpec((D, D), lambda i: (0, 0)),
                               pl.BlockSpec((D, 2 * LANES), lambda i: (0, 0)),
                               pl.BlockSpec((1, LANES), lambda i: (0, 0))],
        out_specs=[row, row, pl.BlockSpec((tm, LANES), lambda i: (i, 0))],
        out_shape=[jax.ShapeDtypeStruct((T, D), F32), jax.ShapeDtypeStruct((T, D), BF16),
                   jax.ShapeDtypeStruct((T, LANES), F32)],
        compiler_params=_cparams(("arbitrary",)),
        name="out_proj_norm_router",
    )(*lead, x, mods, g2.reshape(1, D), wo, rw, rb)


def _moe_body(be_ref, nused_ref, x_ref, win_ref, wout_ref, bin_ref, bout_ref, o_ref, win_b, wout_b, stage, act_scr):
    i = pl.program_id(0)
    D, F2 = win_b.shape
    F = F2 // 2
    rows = x_ref.shape[0]
    used = i < nused_ref[0]
    changed = jnp.logical_or(i == 0, be_ref[i] != be_ref[jnp.maximum(i - 1, 0)])

    @pl.when(jnp.logical_and(used, changed))
    def _():
        def cast_rows(r, carry):
            rs = pl.ds(pl.multiple_of(r * LANES, LANES), LANES)
            win_b[rs, :] = win_ref[0, rs, :].astype(BF16)
            return carry

        lax.fori_loop(0, D // LANES, cast_rows, 0)
        for c in range(D // LANES):
            cs = slice(c * LANES, (c + 1) * LANES)
            stage[c, pl.ds(0, F // 2, stride=2), :] = wout_ref[0, :F // 2, cs]
            stage[c, pl.ds(1, F // 2, stride=2), :] = wout_ref[0, F // 2:, cs]
            wout_b[:, cs] = stage[c].astype(BF16)

    @pl.when(used)
    def _():
        x = x_ref[...]
        even = (lax.broadcasted_iota(jnp.int32, (rows, LANES), 1) % 2) == 0
        cw = 256
        for c0 in range(0, F, cw):
            ha = _dot(x, win_b[:, c0:c0 + cw]) + bin_ref[0, :, c0:c0 + cw]
            hb = _dot(x, win_b[:, F + c0:F + c0 + cw]) + bin_ref[0, :, F + c0:F + c0 + cw]
            for l0 in range(0, cw, LANES):
                a = ha[:, l0:l0 + LANES]
                b = hb[:, l0:l0 + LANES]
                gate = jnp.minimum(jnp.where(even, a, pltpu.roll(b, 1, 1)), SWIGLU_LIMIT)
                up = jnp.clip(jnp.where(even, pltpu.roll(a, LANES - 1, 1), b), -SWIGLU_LIMIT, SWIGLU_LIMIT)
                act = (up + 1.0) * (gate * jax.nn.sigmoid(SWIGLU_ALPHA * gate))
                act_scr[:, c0 + l0:c0 + l0 + LANES] = act.astype(BF16)
        o_ref[...] = (_dot(act_scr[...], wout_b[...]) + bout_ref[0]).astype(o_ref.dtype)

    @pl.when(jnp.logical_not(used))
    def _():
        o_ref[...] = jnp.zeros_like(o_ref)


def _moe_experts(xb, block_expert, n_used, layer, w_in, b_in, w_out, b_out):
    n_rows, D = xb.shape
    n_blocks = n_rows // MOE_BLOCK
    L, E, _, F2 = w_in.shape
    F = F2 // 2
    wspec = lambda shape: pl.BlockSpec((1,) + shape, lambda i, be, nu: (layer * E + be[i], 0, 0))
    grid_spec = pltpu.PrefetchScalarGridSpec(
        num_scalar_prefetch=2,
        grid=(n_blocks,),
        in_specs=[pl.BlockSpec((MOE_BLOCK, D), lambda i, be, nu: (i, 0)),
                  wspec((D, F2)), wspec((F, D)), wspec((1, F2)), wspec((1, D))],
        out_specs=pl.BlockSpec((MOE_BLOCK, D), lambda i, be, nu: (i, 0)),
        scratch_shapes=[pltpu.VMEM((D, F2), BF16), pltpu.VMEM((F, D), BF16),
                        pltpu.VMEM((D // LANES, F, LANES), F32), pltpu.VMEM((MOE_BLOCK, F), BF16)],
    )
    return pl.pallas_call(
        _moe_body,
        grid_spec=grid_spec,
        out_shape=jax.ShapeDtypeStruct((n_rows, D), BF16),
        compiler_params=_cparams(("arbitrary",)),
        name="moe_experts",
    )(block_expert, n_used, xb, w_in.reshape(L * E, D, F2), w_out.reshape(L * E, F, D),
      b_in.reshape(L * E, 1, F2), b_out.reshape(L * E, 1, D))


def _combine_body(g_ref, gate_ref, x_ref, mod_ref, o_ref):
    gates = gate_ref[...]
    y = gates[:, 0:1] * g_ref[0].astype(F32)
    for k in range(1, TOP_K):
        y = y + gates[:, k:k + 1] * g_ref[k].astype(F32)
    o_ref[...] = x_ref[...] + mod_ref[0][5:6, :] * y


def _moe_combine(gathered, gates, x, mods, geom):
    T, D = x.shape
    tm, n_lat_tiles, tiles_per_batch, B = geom
    mod_idx = lambda i: jnp.where(i < n_lat_tiles, i // tiles_per_batch, B)
    return pl.pallas_call(
        _combine_body,
        grid=(T // tm,),
        in_specs=[pl.BlockSpec((TOP_K, tm, D), lambda i: (0, i, 0)),
                  pl.BlockSpec((tm, TOP_K), lambda i: (i, 0)),
                  pl.BlockSpec((tm, D), lambda i: (i, 0)),
                  pl.BlockSpec((1, N_MOD, D), lambda i: (mod_idx(i), 0, 0))],
        out_specs=pl.BlockSpec((tm, D), lambda i: (i, 0)),
        out_shape=jax.ShapeDtypeStruct((T, D), F32),
        compiler_params=_cparams(("arbitrary",)),
        name="moe_combine",
    )(gathered, gates, x, mods)


def _moe_layer(xs, h2, logits, mods, geom, layer, w_in, b_in, w_out, b_out):
    T, D = h2.shape
    top_val, top_idx = lax.top_k(logits[:, :N_EXPERTS], TOP_K)
    gates = jax.nn.softmax(top_val, axis=-1)
    onehot = jnp.sum(jax.nn.one_hot(top_idx, N_EXPERTS, dtype=jnp.int32), axis=1)
    before = jnp.cumsum(onehot, axis=0) - onehot
    counts = jnp.sum(onehot, axis=0)
    starts = jnp.cumsum(counts) - counts
    padded = (counts + MOE_BLOCK - 1) // MOE_BLOCK * MOE_BLOCK
    pad_ends = jnp.cumsum(padded)
    pad_starts = pad_ends - padded
    dest = pad_starts[top_idx] + jnp.take_along_axis(before, top_idx, axis=1)
    n_assign = T * TOP_K
    n_blocks = -(-n_assign // MOE_BLOCK) + N_EXPERTS
    n_rows = n_blocks * MOE_BLOCK
    block_expert = jnp.minimum(
        jnp.searchsorted(pad_ends, jnp.arange(n_blocks) * MOE_BLOCK, side='right'), N_EXPERTS - 1).astype(jnp.int32)
    n_used = (pad_ends[-1] // MOE_BLOCK).astype(jnp.int32).reshape(1)
    order = jnp.argsort(top_idx.reshape(-1), stable=True).astype(jnp.int32)
    row_e = jnp.repeat(block_expert, MOE_BLOCK)
    off = jnp.arange(n_rows, dtype=jnp.int32) - pad_starts[row_e].astype(jnp.int32)
    src_sorted = jnp.clip(starts[row_e].astype(jnp.int32) + off, 0, n_assign - 1)
    src_tok = jnp.where(off < counts[row_e], order[src_sorted] // TOP_K, T)
    x_pad = jnp.concatenate([h2, jnp.zeros((8, D), h2.dtype)], axis=0)
    xb = x_pad.at[src_tok].get(mode='promise_in_bounds')
    yb = _moe_experts(xb, block_expert, n_used, layer, w_in, b_in, w_out, b_out)
    gathered = yb.at[dest.T.reshape(-1)].get(mode='promise_in_bounds').reshape(TOP_K, T, D)
    return _moe_combine(gathered, gates, xs, mods, geom)


def _final_norm_body(x_ref, g_ref, o_ref):
    x = x_ref[...]
    o_ref[...] = x * lax.rsqrt(jnp.mean(x * x, axis=-1, keepdims=True) + NORM_EPS) * g_ref[...]


def _final_norm(x, g, tm):
    T, D = x.shape
    return pl.pallas_call(
        _final_norm_body,
        grid=(T // tm,),
        in_specs=[pl.BlockSpec((tm, D), lambda i: (i, 0)), pl.BlockSpec((1, D), lambda i: (0, 0))],
        out_specs=pl.BlockSpec((tm, D), lambda i: (i, 0)),
        out_shape=jax.ShapeDtypeStruct((T, D), F32),
        compiler_params=_cparams(("arbitrary",)),
        name="final_norm",
    )(x, g.reshape(1, D))


def kernel(x, c, ctx, c_ctx, ada_w, ada_b, norm1_g, norm2_g, a_wqkv, a_wo, a_sink, m_win, m_bgate, m_hnorm,
           m_wo, n_wqkv, n_wo, n_rpb, router_w, router_b, exp_w_in, exp_b_in, exp_w_out, exp_b_out, final_g):
    B, S, D = x.shape
    Lc = ctx.shape[1]
    depth = ada_w.shape[0]
    n_lat, n_ctx = B * S, B * Lc
    assert D == D_MODEL and S % (NA_ROWS * GRID_W) == 0 and Lc % MLSTM_CHUNK == 0 and n_lat % Lc == 0
    tm = _row_tile(S, n_ctx)
    geom = (tm, n_lat // tm, S // tm, B)

    xs = jnp.concatenate([x.reshape(n_lat, D), ctx.reshape(n_ctx, D)], axis=0).astype(F32)
    c_rows = -(-(B + 1) // 8) * 8
    c_all = jnp.zeros((c_rows, D), F32).at[:B].set(c.astype(F32)).at[B].set(c_ctx.astype(F32))
    mods = _ada_mods(c_all, ada_w.astype(F32), ada_b.astype(F32)).reshape(depth, c_rows, N_MOD, D)
    rope = _rope_tables(B, S, n_ctx)

    for layer in range(depth):
        kind = layer % N_MIXERS
        j = layer // N_MIXERS
        mod_l = mods[layer]
        g1 = norm1_g[layer].astype(F32)
        if kind == 0:
            rope_cols = (A_HEADS + A_KV_HEADS) * A_HEAD_DIM
            qkv = _project(xs, mod_l, g1, a_wqkv[j].astype(BF16), geom, BF16, rope, rope_cols // LANES)
            mixed = _window_attention(qkv, a_sink[j].astype(F32), B, S, Lc)
            wo = a_wo[j]
        elif kind == 1:
            n_in = m_win.shape[2]
            n_pad = -(-n_in // LANES) * LANES
            w_in = jnp.zeros((D, n_pad), BF16).at[:, :n_in].set(m_win[j].astype(BF16))
            p = _project(xs, mod_l, g1, w_in, geom, F32)
            gates_t = p[:, n_in - 4 * M_HEADS:n_in].T
            hf, hb = _mlstm_scan(p, gates_t, m_bgate[j], B, S, Lc)
            mixed = (hf, hb, p, m_hnorm[j].astype(F32))
            wo = m_wo[j]
        else:
            qkv = _project(xs, mod_l, g1, n_wqkv[j].astype(BF16), geom, BF16)
            mixed = _neighborhood_attention(qkv, n_rpb[j], B, S, Lc)
            wo = n_wo[j]
        xs, h2, logits = _out_project(mixed, xs, mod_l, norm2_g[layer].astype(F32), wo.astype(BF16),
                                      router_w[layer], router_b[layer], geom, kind == 1)
        xs = _moe_layer(xs, h2, logits, mod_l, geom, layer, exp_w_in, exp_b_in, exp_w_out, exp_b_out)
    return _final_norm(xs[:n_lat], final_g.astype(F32), tm).reshape(B, S, D).astype(x.dtype)
```

```python
import functools

import jax
import jax.numpy as jnp
from jax import lax
from jax.experimental import pallas as pl
from jax.experimental.pallas import tpu as pltpu

F32 = jnp.float32
BF16 = jnp.bfloat16

D_MODEL = 1024
GRID_W = 64
N_MIXERS = 3
N_MOD = 6
NORM_EPS = 1e-6
NEG_INF = -1e30
A_HEADS = 16
A_KV_HEADS = 4
A_HEAD_DIM = D_MODEL // A_HEADS
A_WINDOW = 128
A_BLOCK = 128
ROPE_BASE = 10000.0
M_HEADS = 4
M_V_DIM = D_MODEL // M_HEADS
M_QK_DIM = M_V_DIM // 2
C_HEADS = 16
C_HEAD_DIM = D_MODEL // C_HEADS
NA_WIN_R = 8
NA_WIN_C = 16
N_EXPERTS = 32
TOP_K = 4
D_FF = D_MODEL
SWIGLU_LIMIT = 7.0
SWIGLU_ALPHA = 1.702
MOE_BLOCK = 512

LANES = 128
MLSTM_CHUNK = 128
NA_ROWS = 8
VMEM_LIMIT = 56 * 1024 * 1024


def _cparams(sem):
    return pltpu.CompilerParams(dimension_semantics=sem, vmem_limit_bytes=VMEM_LIMIT)


def _row_tile(S, n_ctx):
    for tm in (512, 256, 128):
        if S % tm == 0 and n_ctx % tm == 0:
            return tm
    raise ValueError("sequence lengths must be multiples of 128")


def _dot(a, b):
    return jnp.dot(a, b, preferred_element_type=F32)


def _dot_nt(a, b):
    return lax.dot_general(a, b, (((1,), (1,)), ((), ())), preferred_element_type=F32)


def _ada_body(c_ref, w_ref, b_ref, o_ref):
    c = c_ref[...]
    s = (c * jax.nn.sigmoid(c)).astype(BF16)
    o_ref[0] = _dot(s, w_ref[0].astype(BF16)) + b_ref[0]


def _ada_mods(c_all, ada_w, ada_b):
    depth, D, N = ada_w.shape
    R = c_all.shape[0]
    tn = 1536
    return pl.pallas_call(
        _ada_body,
        grid=(depth, N // tn),
        in_specs=[pl.BlockSpec((R, D), lambda l, j: (0, 0)),
                  pl.BlockSpec((1, D, tn), lambda l, j: (l, 0, j)),
                  pl.BlockSpec((1, 1, tn), lambda l, j: (l, 0, j))],
        out_specs=pl.BlockSpec((1, R, tn), lambda l, j: (l, 0, j)),
        out_shape=jax.ShapeDtypeStruct((depth, R, N), F32),
        compiler_params=_cparams(("arbitrary", "arbitrary")),
        name="ada_mods",
    )(c_all, ada_w, ada_b.reshape(depth, 1, N))


def _norm_mod(x, g, mod, shift_idx):
    ms = jnp.mean(x * x, axis=-1, keepdims=True)
    y = x * lax.rsqrt(ms + NORM_EPS) * g
    return y * (1.0 + mod[shift_idx + 1:shift_idx + 2, :]) + mod[shift_idx:shift_idx + 1, :]


def _proj_body(x_ref, mod_ref, g_ref, w_ref, *rest, rope_chunks):
    if rope_chunks:
        cos_ref, sin_ref, o_ref = rest
    else:
        (o_ref,) = rest
    h = _norm_mod(x_ref[...], g_ref[...], mod_ref[0], 0).astype(BF16)
    tm = h.shape[0]
    n_out = o_ref.shape[1]
    if rope_chunks:
        first = (lax.broadcasted_iota(jnp.int32, (tm, LANES), 1) % 32) < 16
        cos = cos_ref[...]
        sin = sin_ref[...]
    cw = 512
    for c0 in range(0, n_out, cw):
        c1 = min(c0 + cw, n_out)
        acc = _dot(h, w_ref[:, c0:c1])
        for l0 in range(c0, c1, LANES):
            t = acc[:, l0 - c0:l0 - c0 + LANES]
            if l0 // LANES < rope_chunks:
                partner = jnp.where(first, pltpu.roll(t, LANES - 16, 1), pltpu.roll(t, 16, 1))
                t = t * cos + partner * sin
            o_ref[:, l0:l0 + LANES] = t.astype(o_ref.dtype)


def _project(x, mods, g, w, geom, out_dtype, rope=None, rope_chunks=0):
    T, D = x.shape
    N = w.shape[1]
    tm, n_lat_tiles, tiles_per_batch, B = geom
    mod_idx = lambda i: jnp.where(i < n_lat_tiles, i // tiles_per_batch, B)
    in_specs = [pl.BlockSpec((tm, D), lambda i: (i, 0)),
                pl.BlockSpec((1, N_MOD, D), lambda i: (mod_idx(i), 0, 0)),
                pl.BlockSpec((1, D), lambda i: (0, 0)),
                pl.BlockSpec((D, N), lambda i: (0, 0))]
    args = [x, mods, g.reshape(1, D), w]
    if rope_chunks:
        in_specs += [pl.BlockSpec((tm, LANES), lambda i: (i, 0))] * 2
        args += list(rope)
    return pl.pallas_call(
        functools.partial(_proj_body, rope_chunks=rope_chunks),
        grid=(T // tm,),
        in_specs=in_specs,
        out_specs=pl.BlockSpec((tm, N), lambda i: (i, 0)),
        out_shape=jax.ShapeDtypeStruct((T, N), out_dtype),
        compiler_params=_cparams(("arbitrary",)),
        name="norm_mod_proj",
    )(*args)


def _rope_tables(B, S, n_ctx):
    quarter = A_HEAD_DIM // 4
    freqs = ROPE_BASE ** (-jnp.arange(quarter, dtype=F32) / quarter)
    pos = jnp.arange(S, dtype=jnp.int32)
    ang_r = (pos // GRID_W).astype(F32)[:, None] * freqs[None, :]
    ang_c = (pos % GRID_W).astype(F32)[:, None] * freqs[None, :]
    cos = jnp.concatenate([jnp.cos(ang_r)] * 2 + [jnp.cos(ang_c)] * 2, axis=-1)
    sin = jnp.concatenate([-jnp.sin(ang_r), jnp.sin(ang_r), -jnp.sin(ang_c), jnp.sin(ang_c)], axis=-1)
    cos = jnp.tile(jnp.tile(cos, (1, 2)), (B, 1))
    sin = jnp.tile(jnp.tile(sin, (1, 2)), (B, 1))
    cos = jnp.concatenate([cos, jnp.ones((n_ctx, LANES), F32)], axis=0)
    sin = jnp.concatenate([sin, jnp.zeros((n_ctx, LANES), F32)], axis=0)
    return cos, sin


def _wattn_body(sink_ref, q_ref, *rest, n_blocks, local):
    if local:
        kp_ref, kc_ref, kn_ref, vp_ref, vc_ref, vn_ref, kx_ref, vx_ref, o_ref = rest
    else:
        kx_ref, vx_ref, o_ref = rest
    G = A_HEADS // A_KV_HEADS
    d = A_HEAD_DIM
    blk = A_BLOCK
    scale = d ** -0.5
    n_ctx = kx_ref.shape[0]
    if local:
        i = pl.program_id(1)
        qpos = lax.broadcasted_iota(jnp.int32, (G * blk, 3 * blk), 0) % blk
        koff = lax.broadcasted_iota(jnp.int32, (G * blk, 3 * blk), 1) - blk
        band = jnp.abs(koff - qpos) <= A_WINDOW
        kpos = koff + i * blk
        valid = jnp.logical_and(band, jnp.logical_and(kpos >= 0, kpos < n_blocks * blk))
    heads = [slice(h * d, (h + 1) * d) for h in range(A_KV_HEADS)]
    s_ctx, s_loc = [], []
    for h, cs in enumerate(heads):
        qh = jnp.concatenate([q_ref[:, (h * G + g) * d:(h * G + g + 1) * d] for g in range(G)], axis=0)
        s_ctx.append(_dot_nt(qh, kx_ref[:, cs]) * scale)
        if local:
            k_loc = jnp.concatenate([kp_ref[:, cs], kc_ref[:, cs], kn_ref[:, cs]], axis=0)
            s_loc.append(_dot_nt(qh, k_loc) * scale)
    p_ctx, p_loc, inv_den = [], [], []
    for h in range(A_KV_HEADS):
        sink = jnp.concatenate([jnp.full((blk, 1), sink_ref[h * G + g], F32) for g in range(G)], axis=0)
        m = jnp.maximum(jnp.max(s_ctx[h], axis=-1, keepdims=True), sink)
        if local:
            sl = jnp.where(valid, s_loc[h], NEG_INF)
            m = jnp.maximum(m, jnp.max(sl, axis=-1, keepdims=True))
        pc = jnp.exp(s_ctx[h] - m)
        den = jnp.sum(pc, axis=-1, keepdims=True) + jnp.exp(sink - m)
        p_ctx.append(pc.astype(BF16))
        if local:
            pl_ = jnp.exp(sl - m)
            den = den + jnp.sum(pl_, axis=-1, keepdims=True)
            p_loc.append(pl_.astype(BF16))
        inv_den.append(1.0 / den)
    outs = [None] * A_HEADS
    for h, cs in enumerate(heads):
        o = _dot(p_ctx[h], vx_ref[:, cs])
        if local:
            v_loc = jnp.concatenate([vp_ref[:, cs], vc_ref[:, cs], vn_ref[:, cs]], axis=0)
            o = o + _dot(p_loc[h], v_loc)
        o = o * inv_den[h]
        for g in range(G):
            outs[h * G + g] = o[g * blk:(g + 1) * blk, :]
    o_ref[...] = jnp.concatenate(outs, axis=-1).astype(o_ref.dtype)


def _window_attention(qkv, sink, B, S, n_ctx_per):
    T = qkv.shape[0]
    blk = A_BLOCK
    qw = A_HEADS * A_HEAD_DIM
    kw = A_KV_HEADS * A_HEAD_DIM
    nb = S // blk
    kcol = qw // kw
    vcol = kcol + 1
    ctx_blk0 = (B * S) // n_ctx_per
    smem = pl.BlockSpec(memory_space=pltpu.SMEM)
    kx = pl.BlockSpec((n_ctx_per, kw), lambda b, i: (ctx_blk0 + b, kcol))
    vx = pl.BlockSpec((n_ctx_per, kw), lambda b, i: (ctx_blk0 + b, vcol))

    def nbr(col, delta):
        return pl.BlockSpec((blk, kw), lambda b, i: (b * nb + jnp.clip(i + delta, 0, nb - 1), col))

    lat = pl.pallas_call(
        functools.partial(_wattn_body, n_blocks=nb, local=True),
        grid=(B, nb),
        in_specs=[smem, pl.BlockSpec((blk, qw), lambda b, i: (b * nb + i, 0)),
                  nbr(kcol, -1), nbr(kcol, 0), nbr(kcol, 1),
                  nbr(vcol, -1), nbr(vcol, 0), nbr(vcol, 1), kx, vx],
        out_specs=pl.BlockSpec((blk, qw), lambda b, i: (b * nb + i, 0)),
        out_shape=jax.ShapeDtypeStruct((B * S, qw), BF16),
        compiler_params=_cparams(("arbitrary", "arbitrary")),
        name="window_attn",
    )(sink, qkv, qkv, qkv, qkv, qkv, qkv, qkv, qkv, qkv)
    ncb = n_ctx_per // blk
    q0 = (B * S) // blk
    ctx = pl.pallas_call(
        functools.partial(_wattn_body, n_blocks=ncb, local=False),
        grid=(B, ncb),
        in_specs=[smem, pl.BlockSpec((blk, qw), lambda b, i: (q0 + b * ncb + i, 0)), kx, vx],
        out_specs=pl.BlockSpec((blk, qw), lambda b, i: (b * ncb + i, 0)),
        out_shape=jax.ShapeDtypeStruct((B * n_ctx_per, qw), BF16),
        compiler_params=_cparams(("arbitrary", "arbitrary")),
        name="window_attn_ctx",
    )(sink, qkv, qkv, qkv)
    return jnp.concatenate([lat, ctx], axis=0)


def _nattn_body(q_ref, *rest, n_img_rows, local):
    if local:
        kp_ref, kc_ref, kn_ref, vp_ref, vc_ref, vn_ref, kx_ref, vx_ref, bias_ref, o_ref, kbuf, vbuf = rest
    else:
        kx_ref, vx_ref, o_ref = rest
    d = C_HEAD_DIM
    scale = d ** -0.5
    W = GRID_W
    n_keys = NA_WIN_R * W
    rows_per_step = q_ref.shape[0] // W

    def one_row(rr, off, dr0):
        qrow = q_ref[pl.ds(pl.multiple_of(rr * W, W), W), :]
        pairs = [slice(p * LANES, (p + 1) * LANES) for p in range(C_HEADS * d // LANES)]
        first = lax.broadcasted_iota(jnp.int32, (W, LANES), 1) < d
        zero = jnp.zeros((W, LANES), BF16)
        q_heads = []
        for ps in pairs:
            q_heads += [jnp.where(first, qrow[:, ps], zero), jnp.where(first, zero, qrow[:, ps])]
        s_ctx = [_dot_nt(q_heads[h], kx_ref[:, pairs[h // 2]]) * scale for h in range(C_HEADS)]
        if local:
            s_loc = [_dot_nt(q_heads[h], kbuf[pl.ds(off, n_keys), pairs[h // 2]]) * scale for h in range(C_HEADS)]
        p_ctx, p_loc, inv_den = [], [], []
        for h in range(C_HEADS):
            m = jnp.max(s_ctx[h], axis=-1, keepdims=True)
            if local:
                bias = jnp.concatenate(
                    [bias_ref[h, pl.ds(dr0 + 2 * t, 1)][0] for t in range(NA_WIN_R // 2)], axis=-1)
                sl = s_loc[h] + bias
                m = jnp.maximum(m, jnp.max(sl, axis=-1, keepdims=True))
            pc = jnp.exp(s_ctx[h] - m)
            den = jnp.sum(pc, axis=-1, keepdims=True)
            p_ctx.append(pc.astype(BF16))
            if local:
                pl_ = jnp.exp(sl - m)
                den = den + jnp.sum(pl_, axis=-1, keepdims=True)
                p_loc.append(pl_.astype(BF16))
            inv_den.append(1.0 / den)
        o_heads = []
        for h in range(C_HEADS):
            o = _dot(p_ctx[h], vx_ref[:, pairs[h // 2]])
            if local:
                o = o + _dot(p_loc[h], vbuf[pl.ds(off, n_keys), pairs[h // 2]])
            o_heads.append(o * inv_den[h])
        outs = [jnp.where(first, o_heads[2 * p], o_heads[2 * p + 1]) for p in range(len(pairs))]
        o_ref[pl.ds(pl.multiple_of(rr * W, W), W), :] = jnp.concatenate(outs, axis=-1).astype(o_ref.dtype)

    if local:
        j = pl.program_id(1)
        blk_tokens = rows_per_step * W
        for t, (kr, vr) in enumerate(((kp_ref, vp_ref), (kc_ref, vc_ref), (kn_ref, vn_ref))):
            kbuf[t * blk_tokens:(t + 1) * blk_tokens, :] = kr[...]
            vbuf[t * blk_tokens:(t + 1) * blk_tokens, :] = vr[...]

        def body(rr, carry):
            r = j * rows_per_step + rr
            rs = jnp.clip(r - NA_WIN_R // 2, 0, n_img_rows - NA_WIN_R)
            off = pl.multiple_of((rs - (j - 1) * rows_per_step) * W, W)
            one_row(rr, off, rs - r + NA_WIN_R - 1)
            return carry

        lax.fori_loop(0, rows_per_step, body, 0)
    else:
        def body(rr, carry):
            one_row(rr, None, None)
            return carry

        lax.fori_loop(0, rows_per_step, body, 0)


def _na_bias_table(rpb):
    W = GRID_W
    qcol = jnp.arange(W)
    kcol = jnp.arange(W)
    cstart = jnp.clip(qcol - NA_WIN_C // 2, 0, W - NA_WIN_C)
    col_in = (kcol[None, :] >= cstart[:, None]) & (kcol[None, :] < cstart[:, None] + NA_WIN_C)
    dc_idx = jnp.clip(kcol[None, :] - qcol[:, None] + NA_WIN_C - 1, 0, 2 * NA_WIN_C - 2)
    t2 = rpb.astype(F32)[:, :, dc_idx]
    t2 = jnp.where(col_in[None, None], t2, NEG_INF)
    return jnp.concatenate([t2[:, :-1], t2[:, 1:]], axis=-1)


def _neighborhood_attention(qkv, rpb, B, S, n_ctx_per):
    hw = C_HEADS * C_HEAD_DIM
    R = S // GRID_W
    rows = NA_ROWS
    blk = rows * GRID_W
    nj = R // rows
    ctx_blk0 = (B * S) // n_ctx_per
    kx = pl.BlockSpec((n_ctx_per, hw), lambda b, j: (ctx_blk0 + b, 1))
    vx = pl.BlockSpec((n_ctx_per, hw), lambda b, j: (ctx_blk0 + b, 2))
    bias = _na_bias_table(rpb)

    def nbr(col, delta):
        return pl.BlockSpec((blk, hw), lambda b, j: (b * nj + jnp.clip(j + delta, 0, nj - 1), col))

    lat = pl.pallas_call(
        functools.partial(_nattn_body, n_img_rows=R, local=True),
        grid=(B, nj),
        in_specs=[pl.BlockSpec((blk, hw), lambda b, j: (b * nj + j, 0)),
                  nbr(1, -1), nbr(1, 0), nbr(1, 1), nbr(2, -1), nbr(2, 0), nbr(2, 1), kx, vx,
                  pl.BlockSpec(bias.shape, lambda b, j: (0, 0, 0, 0))],
        out_specs=pl.BlockSpec((blk, hw), lambda b, j: (b * nj + j, 0)),
        out_shape=jax.ShapeDtypeStruct((B * S, hw), BF16),
        scratch_shapes=[pltpu.VMEM((3 * blk, hw), BF16), pltpu.VMEM((3 * blk, hw), BF16)],
        compiler_params=_cparams(("arbitrary", "arbitrary")),
        name="neighborhood_attn",
    )(qkv, qkv, qkv, qkv, qkv, qkv, qkv, qkv, qkv, bias)
    q0 = (B * S) // n_ctx_per
    ctx = pl.pallas_call(
        functools.partial(_nattn_body, n_img_rows=R, local=False),
        grid=(B, 1),
        in_specs=[pl.BlockSpec((n_ctx_per, hw), lambda b, j: (q0 + b, 0)), kx, vx],
        out_specs=pl.BlockSpec((n_ctx_per, hw), lambda b, j: (b, 0)),
        out_shape=jax.ShapeDtypeStruct((B * n_ctx_per, hw), BF16),
        compiler_params=_cparams(("arbitrary", "arbitrary")),
        name="neighborhood_attn_ctx",
    )(qkv, qkv, qkv)
    return jnp.concatenate([lat, ctx], axis=0)


def _log_sigmoid(x):
    return jnp.minimum(x, 0.0) - jnp.log1p(jnp.exp(-jnp.abs(x)))


def _mlstm_body(qf, kf, vf, gcf, grf, qb, kb, vb, gcb, grb, bgr_ref, bgc_ref, hf_ref, hb_ref, st_ref, m_ref):
    ch = qf.shape[0]
    dk, dv = M_QK_DIM, M_V_DIM
    H = M_HEADS

    @pl.when(pl.program_id(1) == 0)
    def _():
        st_ref[...] = jnp.zeros_like(st_ref)
        m_ref[...] = jnp.full_like(m_ref, NEG_INF)

    row_i = lax.broadcasted_iota(jnp.int32, (ch, ch), 0)
    col_i = lax.broadcasted_iota(jnp.int32, (ch, ch), 1)
    lower = row_i >= col_i
    upper = row_i <= col_i
    tril = lower.astype(F32)
    triu = upper.astype(F32)
    hi = lax.Precision.HIGHEST
    dirs = ((qf, kf, vf, gcf, grf, hf_ref), (qb, kb, vb, gcb, grb, hb_ref))
    chains = []
    for direction, (q_ref, k_ref, v_ref, gc_ref, gr_ref, o_ref) in enumerate(dirs):
        fwd = direction == 0
        gcol = gc_ref[...] + bgr_ref[...]
        grow = gr_ref[...] + bgc_ref[...]
        cum_c = jnp.dot(tril if fwd else triu, _log_sigmoid(gcol), precision=hi, preferred_element_type=F32)
        cum_r = jnp.dot(_log_sigmoid(grow), triu if fwd else tril, precision=hi, preferred_element_type=F32)
        for h in range(H):
            c = direction * H + h
            q = q_ref[:, h * dk:(h + 1) * dk].astype(BF16)
            k_s = k_ref[:, h * dk:(h + 1) * dk] * (dk ** -0.5)
            state = st_ref[c]
            chains.append(dict(c=c, h=h, fwd=fwd, gcol=gcol, grow=grow, cum_c=cum_c, cum_r=cum_r, state=state,
                               k_t=k_s.T.astype(BF16), v=v_ref[:, h * dv:(h + 1) * dv], o_ref=o_ref,
                               s_raw=_dot_nt(q, k_s.astype(BF16)), q_state=_dot(q, state.astype(BF16))))
    for ck in chains:
        h, fwd = ck["h"], ck["fwd"]
        gi = (0 if fwd else 2 * H) + h
        gf = gi + H
        b_col = ck["cum_c"][:, gf:gf + 1]
        b_row = ck["cum_r"][gf:gf + 1, :]
        i_col = ck["gcol"][:, gi:gi + 1]
        i_row = ck["grow"][gi:gi + 1, :]
        b_last = b_row[:, ch - 1:ch] if fwd else b_row[:, 0:1]
        m_prev = m_ref[ck["c"]][:, 0:1]
        log_d = jnp.where(lower if fwd else upper, b_col - b_row + i_row, NEG_INF)
        m_inter = b_col + m_prev
        m_t = jnp.maximum(jnp.max(log_d, axis=-1, keepdims=True), m_inter)
        s = ck["s_raw"] * jnp.exp(log_d - m_t)
        m_new = jnp.maximum(b_last + m_prev, jnp.max(b_last - b_row + i_row, axis=-1, keepdims=True))
        w_col = jnp.exp(b_last - b_col + i_col - m_new)
        v = ck["v"]
        wv = jnp.concatenate([v * w_col, jnp.broadcast_to(w_col, (ch, LANES))], axis=-1).astype(BF16)
        ck.update(m_t=m_t, inter=jnp.exp(m_inter - m_t), m_new=m_new, decay=jnp.exp(b_last + m_prev - m_new),
                  s_sum=jnp.sum(s, axis=-1, keepdims=True), sv=_dot(s.astype(BF16), v.astype(BF16)),
                  upd=_dot(ck["k_t"], wv))
    for ck in chains:
        h, c = ck["h"], ck["c"]
        num = ck["sv"] + ck["inter"] * ck["q_state"][:, :dv]
        den = ck["s_sum"] + ck["inter"] * ck["q_state"][:, dv:dv + 1]
        ck["o_ref"][:, h * dv:(h + 1) * dv] = num * (1.0 / jnp.maximum(jnp.abs(den), jnp.exp(-ck["m_t"])))
        st_ref[c] = ck["decay"] * ck["state"] + ck["upd"]
        m_ref[c] = jnp.broadcast_to(ck["m_new"], (1, LANES))


def _mlstm_scan(p, gates_t, b_gate, B, S, n_ctx_per):
    T = p.shape[0]
    ch = MLSTM_CHUNK
    H = M_HEADS
    vw = H * M_V_DIM
    ncc = n_ctx_per // ch
    nlc = S // ch
    gate_col = (2 * H * M_QK_DIM + 2 * vw) // LANES

    def fwd_idx(b, s):
        return jnp.where(s < ncc, (B * S + b * n_ctx_per) // ch + s, b * nlc + s - ncc)

    def bwd_idx(b, s):
        return jnp.where(s < ncc, (B * S + b * n_ctx_per) // ch + ncc - 1 - s, b * nlc + nlc - 1 - (s - ncc))

    def specs(idx):
        return [pl.BlockSpec((ch, H * M_QK_DIM), lambda b, s: (idx(b, s), 0)),
                pl.BlockSpec((ch, H * M_QK_DIM), lambda b, s: (idx(b, s), 1)),
                pl.BlockSpec((ch, vw), lambda b, s: (idx(b, s), 1)),
                pl.BlockSpec((ch, LANES), lambda b, s: (idx(b, s), gate_col)),
                pl.BlockSpec((4 * H, ch), lambda b, s: (0, idx(b, s)))]

    bg_row = jnp.zeros((1, LANES), F32).at[0, :4 * H].set(b_gate.astype(F32))
    bg_col = b_gate.astype(F32).reshape(4 * H, 1)
    return pl.pallas_call(
        _mlstm_body,
        grid=(B, ncc + nlc),
        in_specs=specs(fwd_idx) + specs(bwd_idx) + [pl.BlockSpec((1, LANES), lambda b, s: (0, 0)),
                                                     pl.BlockSpec((4 * H, 1), lambda b, s: (0, 0))],
        out_specs=[pl.BlockSpec((ch, vw), lambda b, s: (fwd_idx(b, s), 0)),
                   pl.BlockSpec((ch, vw), lambda b, s: (bwd_idx(b, s), 0))],
        out_shape=[jax.ShapeDtypeStruct((T, vw), F32)] * 2,
        scratch_shapes=[pltpu.VMEM((2 * H, M_QK_DIM, M_V_DIM + LANES), F32),
                        pltpu.VMEM((2 * H, 1, LANES), F32)],
        compiler_params=_cparams(("arbitrary", "arbitrary")),
        name="mlstm_scan",
    )(p, p, p, p, gates_t, p, p, p, p, gates_t, bg_row, bg_col)


def _outproj_body(*refs, mlstm):
    if mlstm:
        hf_ref, hb_ref, og_ref, hn_ref, x_ref, mod_ref, g2_ref, wo_ref, rw_ref, rb_ref, xo_ref, h2_ref, lg_ref = refs
        dv = M_V_DIM
        hsum = hf_ref[...] + hb_ref[...]
        parts = []
        for h in range(M_HEADS):
            t = hsum[:, h * dv:(h + 1) * dv]
            parts.append(t * lax.rsqrt(jnp.mean(t * t, axis=-1, keepdims=True) + NORM_EPS))
        a = (jnp.concatenate(parts, axis=-1) * hn_ref[...] * jax.nn.sigmoid(og_ref[...])).astype(BF16)
    else:
        a_ref, x_ref, mod_ref, g2_ref, wo_ref, rw_ref, rb_ref, xo_ref, h2_ref, lg_ref = refs
        a = a_ref[...]
    mod = mod_ref[0]
    x = x_ref[...] + mod[2:3, :] * _dot(a, wo_ref[...])
    xo_ref[...] = x
    h2 = _norm_mod(x, g2_ref[...], mod, 3)
    h2_hi = h2.astype(BF16)
    h2_ref[...] = h2_hi
    h2_lo = (h2 - h2_hi.astype(F32)).astype(BF16)
    t = _dot(h2_hi, rw_ref[...])
    lg_ref[...] = t[:, :LANES] + (t[:, LANES:] + _dot(h2_lo, rw_ref[:, :LANES])) + rb_ref[...]


def _out_project(mixer_out, x, mods, g2, wo, router_w, router_b, geom, mlstm):
    T, D = x.shape
    tm, n_lat_tiles, tiles_per_batch, B = geom
    mod_idx = lambda i: jnp.where(i < n_lat_tiles, i // tiles_per_batch, B)
    row = pl.BlockSpec((tm, D), lambda i: (i, 0))
    vec = pl.BlockSpec((1, D), lambda i: (0, 0))
    rw = jnp.zeros((D, LANES), F32).at[:, :N_EXPERTS].set(router_w.astype(F32))
    rw_hi = rw.astype(BF16)
    rw = jnp.concatenate([rw_hi, (rw - rw_hi.astype(F32)).astype(BF16)], axis=1)
    rb = jnp.zeros((1, LANES), F32).at[0, :N_EXPERTS].set(router_b.astype(F32))
    if mlstm:
        hf, hb, p, hnorm = mixer_out
        lead_specs = [row, row, pl.BlockSpec((tm, D), lambda i: (i, 2)), vec]
        lead = [hf, hb, p, hnorm.reshape(1, D)]
    else:
        lead_specs = [row]
        lead = [mixer_out]
    return pl.pallas_call(
        functools.partial(_outproj_body, mlstm=mlstm),
        grid=(T // tm,),
        in_specs=lead_specs + [row, pl.BlockSpec((1, N_MOD, D), lambda i: (mod_idx(i), 0, 0)), vec,
                               pl.BlockSpec((D, D), lambda i: (0, 0)),
                               pl.BlockSpec((D, 2 * LANES), lambda i: (0, 0)),
                               pl.BlockSpec((1, LANES), lambda i: (0, 0))],
        out_specs=[row, row, pl.BlockSpec((tm, LANES), lambda i: (i, 0))],
        out_shape=[jax.ShapeDtypeStruct((T, D), F32), jax.ShapeDtypeStruct((T, D), BF16),
                   jax.ShapeDtypeStruct((T, LANES), F32)],
        compiler_params=_cparams(("arbitrary",)),
        name="out_proj_norm_router",
    )(*lead, x, mods, g2.reshape(1, D), wo, rw, rb)


def _moe_body(be_ref, nused_ref, x_ref, win_ref, wout_ref, bin_ref, bout_ref, o_ref, win_b, wout_b, stage, act_scr):
    i = pl.program_id(0)
    D, F2 = win_b.shape
    F = F2 // 2
    rows = x_ref.shape[0]
    used = i < nused_ref[0]
    changed = jnp.logical_or(i == 0, be_ref[i] != be_ref[jnp.maximum(i - 1, 0)])

    @pl.when(jnp.logical_and(used, changed))
    def _():
        def cast_rows(r, carry):
            rs = pl.ds(pl.multiple_of(r * LANES, LANES), LANES)
            win_b[rs, :] = win_ref[0, rs, :].astype(BF16)
            return carry

        lax.fori_loop(0, D // LANES, cast_rows, 0)
        for c in range(D // LANES):
            cs = slice(c * LANES, (c + 1) * LANES)
            stage[c, pl.ds(0, F // 2, stride=2), :] = wout_ref[0, :F // 2, cs]
            stage[c, pl.ds(1, F // 2, stride=2), :] = wout_ref[0, F // 2:, cs]
            wout_b[:, cs] = stage[c].astype(BF16)

    @pl.when(used)
    def _():
        x = x_ref[...]
        even = (lax.broadcasted_iota(jnp.int32, (rows, LANES), 1) % 2) == 0
        cw = 256
        for c0 in range(0, F, cw):
            ha = _dot(x, win_b[:, c0:c0 + cw]) + bin_ref[0, :, c0:c0 + cw]
            hb = _dot(x, win_b[:, F + c0:F + c0 + cw]) + bin_ref[0, :, F + c0:F + c0 + cw]
            for l0 in range(0, cw, LANES):
                a = ha[:, l0:l0 + LANES]
                b = hb[:, l0:l0 + LANES]
                gate = jnp.minimum(jnp.where(even, a, pltpu.roll(b, 1, 1)), SWIGLU_LIMIT)
                up = jnp.clip(jnp.where(even, pltpu.roll(a, LANES - 1, 1), b), -SWIGLU_LIMIT, SWIGLU_LIMIT)
                act = (up + 1.0) * (gate * jax.nn.sigmoid(SWIGLU_ALPHA * gate))
                act_scr[:, c0 + l0:c0 + l0 + LANES] = act.astype(BF16)
        o_ref[...] = (_dot(act_scr[...], wout_b[...]) + bout_ref[0]).astype(o_ref.dtype)

    @pl.when(jnp.logical_not(used))
    def _():
        o_ref[...] = jnp.zeros_like(o_ref)


def _moe_experts(xb, block_expert, n_used, layer, w_in, b_in, w_out, b_out):
    n_rows, D = xb.shape
    n_blocks = n_rows // MOE_BLOCK
    L, E, _, F2 = w_in.shape
    F = F2 // 2
    wspec = lambda shape: pl.BlockSpec((1,) + shape, lambda i, be, nu: (layer * E + be[i], 0, 0))
    grid_spec = pltpu.PrefetchScalarGridSpec(
        num_scalar_prefetch=2,
        grid=(n_blocks,),
        in_specs=[pl.BlockSpec((MOE_BLOCK, D), lambda i, be, nu: (i, 0)),
                  wspec((D, F2)), wspec((F, D)), wspec((1, F2)), wspec((1, D))],
        out_specs=pl.BlockSpec((MOE_BLOCK, D), lambda i, be, nu: (i, 0)),
        scratch_shapes=[pltpu.VMEM((D, F2), BF16), pltpu.VMEM((F, D), BF16),
                        pltpu.VMEM((D // LANES, F, LANES), F32), pltpu.VMEM((MOE_BLOCK, F), BF16)],
    )
    return pl.pallas_call(
        _moe_body,
        grid_spec=grid_spec,
        out_shape=jax.ShapeDtypeStruct((n_rows, D), F32),
        compiler_params=_cparams(("arbitrary",)),
        name="moe_experts",
    )(block_expert, n_used, xb, w_in.reshape(L * E, D, F2), w_out.reshape(L * E, F, D),
      b_in.reshape(L * E, 1, F2), b_out.reshape(L * E, 1, D))


COMBINE_TILE = 256
COMBINE_UNROLL = 8


def _combine_body(idx_ref, idx_next_ref, gate_ref, x_ref, mod_ref, yb_hbm, o_ref, buf, sem):
    i = pl.program_id(0)
    n = pl.num_programs(0)
    tm = x_ref.shape[0]
    n_rows = TOP_K * tm
    slot = i % 2

    def fetch_rows(rows_ref, into):
        def body(j, carry):
            for u in range(COMBINE_UNROLL):
                r = j * COMBINE_UNROLL + u
                pltpu.make_async_copy(yb_hbm.at[pl.ds(rows_ref[r], 1)], buf.at[into, pl.ds(r, 1)],
                                      sem.at[into]).start(priority=u % 2)
            return carry

        lax.fori_loop(0, n_rows // COMBINE_UNROLL, body, 0)

    @pl.when(i == 0)
    def _():
        fetch_rows(idx_ref, 0)

    @pl.when(i + 1 < n)
    def _():
        fetch_rows(idx_next_ref, 1 - slot)

    pltpu.make_async_copy(yb_hbm.at[pl.ds(0, n_rows)], buf.at[slot], sem.at[slot]).wait()
    gates = gate_ref[...]
    y = gates[:, 0:1] * buf[slot, 0:tm, :]
    for k in range(1, TOP_K):
        y = y + gates[:, k:k + 1] * buf[slot, k * tm:(k + 1) * tm, :]
    o_ref[...] = x_ref[...] + mod_ref[0][5:6, :] * y


def _moe_combine(yb, dest, gates, x, mods, geom):
    T, D = x.shape
    _, n_lat_tiles, tiles_per_batch, B = geom
    tm = COMBINE_TILE
    per = geom[0] // tm
    mod_idx = lambda i: jnp.where(i // per < n_lat_tiles, (i // per) // tiles_per_batch, B)
    n_tiles = T // tm
    rows = dest.reshape(n_tiles, tm, TOP_K).transpose(0, 2, 1).reshape(-1).astype(jnp.int32)
    nr = TOP_K * tm
    return pl.pallas_call(
        _combine_body,
        grid=(n_tiles,),
        in_specs=[pl.BlockSpec((nr,), lambda i: (i,), memory_space=pltpu.SMEM),
                  pl.BlockSpec((nr,), lambda i: (jnp.minimum(i + 1, n_tiles - 1),), memory_space=pltpu.SMEM),
                  pl.BlockSpec((tm, TOP_K), lambda i: (i, 0)),
                  pl.BlockSpec((tm, D), lambda i: (i, 0)),
                  pl.BlockSpec((1, N_MOD, D), lambda i: (mod_idx(i), 0, 0)),
                  pl.BlockSpec(memory_space=pl.ANY)],
        out_specs=pl.BlockSpec((tm, D), lambda i: (i, 0)),
        out_shape=jax.ShapeDtypeStruct((T, D), F32),
        scratch_shapes=[pltpu.VMEM((2, nr, D), F32), pltpu.SemaphoreType.DMA((2,))],
        compiler_params=_cparams(("arbitrary",)),
        name="moe_combine",
    )(rows, rows, gates, x, mods, yb)


def _moe_layer(xs, h2, logits, mods, geom, layer, w_in, b_in, w_out, b_out):
    T, D = h2.shape
    top_val, top_idx = lax.top_k(logits[:, :N_EXPERTS], TOP_K)
    gates = jax.nn.softmax(top_val, axis=-1)
    onehot = jnp.sum(jax.nn.one_hot(top_idx, N_EXPERTS, dtype=F32), axis=1)
    seg = LANES
    oh = onehot.reshape(T // seg, seg, N_EXPERTS)
    strict_lower = jnp.tril(jnp.ones((seg, seg), F32), -1)
    within = jnp.einsum('ij,sje->sie', strict_lower, oh, precision=lax.Precision.HIGHEST)
    seg_tot = jnp.sum(oh, axis=1)
    seg_off = jnp.cumsum(seg_tot, axis=0) - seg_tot
    before = (within + seg_off[:, None, :]).reshape(T, N_EXPERTS).astype(jnp.int32)
    counts = jnp.sum(seg_tot, axis=0).astype(jnp.int32)
    starts = jnp.cumsum(counts) - counts
    padded = (counts + MOE_BLOCK - 1) // MOE_BLOCK * MOE_BLOCK
    pad_ends = jnp.cumsum(padded)
    pad_starts = pad_ends - padded
    dest = pad_starts[top_idx] + jnp.take_along_axis(before, top_idx, axis=1)
    n_assign = T * TOP_K
    n_blocks = -(-n_assign // MOE_BLOCK) + N_EXPERTS
    n_rows = n_blocks * MOE_BLOCK
    block_start = jnp.arange(n_blocks, dtype=jnp.int32) * MOE_BLOCK
    block_expert = jnp.minimum(jnp.sum((pad_ends[None, :] <= block_start[:, None]).astype(jnp.int32), axis=1),
                               N_EXPERTS - 1)
    n_used = (pad_ends[-1] // MOE_BLOCK).astype(jnp.int32).reshape(1)
    order = jnp.argsort(top_idx.reshape(-1), stable=True).astype(jnp.int32)
    row_e = jnp.repeat(block_expert, MOE_BLOCK)
    off = jnp.arange(n_rows, dtype=jnp.int32) - pad_starts[row_e].astype(jnp.int32)
    src_sorted = jnp.clip(starts[row_e].astype(jnp.int32) + off, 0, n_assign - 1)
    src_tok = jnp.where(off < counts[row_e], order[src_sorted] // TOP_K, 0)
    xb = h2.at[src_tok].get(mode='promise_in_bounds')
    yb = _moe_experts(xb, block_expert, n_used, layer, w_in, b_in, w_out, b_out)
    return _moe_combine(yb, dest, gates, xs, mods, geom)


def _final_norm_body(x_ref, g_ref, o_ref):
    x = x_ref[...]
    o_ref[...] = x * lax.rsqrt(jnp.mean(x * x, axis=-1, keepdims=True) + NORM_EPS) * g_ref[...]


def _final_norm(x, g, n_rows, tm):
    D = x.shape[1]
    return pl.pallas_call(
        _final_norm_body,
        grid=(n_rows // tm,),
        in_specs=[pl.BlockSpec((tm, D), lambda i: (i, 0)), pl.BlockSpec((1, D), lambda i: (0, 0))],
        out_specs=pl.BlockSpec((tm, D), lambda i: (i, 0)),
        out_shape=jax.ShapeDtypeStruct((n_rows, D), F32),
        compiler_params=_cparams(("arbitrary",)),
        name="final_norm",
    )(x, g.reshape(1, D))


def kernel(x, c, ctx, c_ctx, ada_w, ada_b, norm1_g, norm2_g, a_wqkv, a_wo, a_sink, m_win, m_bgate, m_hnorm,
           m_wo, n_wqkv, n_wo, n_rpb, router_w, router_b, exp_w_in, exp_b_in, exp_w_out, exp_b_out, final_g):
    B, S, D = x.shape
    Lc = ctx.shape[1]
    depth = ada_w.shape[0]
    n_lat, n_ctx = B * S, B * Lc
    assert D == D_MODEL and S % (NA_ROWS * GRID_W) == 0 and Lc % MLSTM_CHUNK == 0 and n_lat % Lc == 0
    tm = _row_tile(S, n_ctx)
    geom = (tm, n_lat // tm, S // tm, B)

    xs = jnp.concatenate([x.reshape(n_lat, D), ctx.reshape(n_ctx, D)], axis=0).astype(F32)
    c_rows = -(-(B + 1) // 8) * 8
    c_all = jnp.zeros((c_rows, D), F32).at[:B].set(c.astype(F32)).at[B].set(c_ctx.astype(F32))
    mods = _ada_mods(c_all, ada_w.astype(F32), ada_b.astype(F32)).reshape(depth, c_rows, N_MOD, D)
    rope = _rope_tables(B, S, n_ctx)

    for layer in range(depth):
        kind = layer % N_MIXERS
        j = layer // N_MIXERS
        mod_l = mods[layer]
        g1 = norm1_g[layer].astype(F32)
        if kind == 0:
            rope_cols = (A_HEADS + A_KV_HEADS) * A_HEAD_DIM
            qkv = _project(xs, mod_l, g1, a_wqkv[j].astype(BF16), geom, BF16, rope, rope_cols // LANES)
            mixed = _window_attention(qkv, a_sink[j].astype(F32), B, S, Lc)
            wo = a_wo[j]
        elif kind == 1:
            n_in = m_win.shape[2]
            n_pad = -(-n_in // LANES) * LANES
            w_in = jnp.zeros((D, n_pad), BF16).at[:, :n_in].set(m_win[j].astype(BF16))
            p = _project(xs, mod_l, g1, w_in, geom, F32)
            gates_t = p[:, n_in - 4 * M_HEADS:n_in].T
            hf, hb = _mlstm_scan(p, gates_t, m_bgate[j], B, S, Lc)
            mixed = (hf, hb, p, m_hnorm[j].astype(F32))
            wo = m_wo[j]
        else:
            qkv = _project(xs, mod_l, g1, n_wqkv[j].astype(BF16), geom, BF16)
            mixed = _neighborhood_attention(qkv, n_rpb[j], B, S, Lc)
            wo = n_wo[j]
        xs, h2, logits = _out_project(mixed, xs, mod_l, norm2_g[layer].astype(F32), wo.astype(BF16),
                                      router_w[layer], router_b[layer], geom, kind == 1)
        xs = _moe_layer(xs, h2, logits, mod_l, geom, layer, exp_w_in, exp_b_in, exp_w_out, exp_b_out)
    return _final_norm(xs, final_g.astype(F32), n_lat, tm).reshape(B, S, D).astype(x.dtype)
```

```python
import functools

import jax
import jax.numpy as jnp
from jax import lax
from jax.experimental import pallas as pl
from jax.experimental.pallas import tpu as pltpu

F32 = jnp.float32
BF16 = jnp.bfloat16

D_MODEL = 1024
GRID_W = 64
N_MIXERS = 3
N_MOD = 6
NORM_EPS = 1e-6
NEG_INF = -1e30
A_HEADS = 16
A_KV_HEADS = 4
A_HEAD_DIM = D_MODEL // A_HEADS
A_WINDOW = 128
A_BLOCK = 128
ROPE_BASE = 10000.0
M_HEADS = 4
M_V_DIM = D_MODEL // M_HEADS
M_QK_DIM = M_V_DIM // 2
C_HEADS = 16
C_HEAD_DIM = D_MODEL // C_HEADS
NA_WIN_R = 8
NA_WIN_C = 16
N_EXPERTS = 32
TOP_K = 4
D_FF = D_MODEL
SWIGLU_LIMIT = 7.0
SWIGLU_ALPHA = 1.702
MOE_BLOCK = 512

LANES = 128
MLSTM_CHUNK = 128
NA_ROWS = 8
VMEM_LIMIT = 56 * 1024 * 1024


def _cparams(sem):
    return pltpu.CompilerParams(dimension_semantics=sem, vmem_limit_bytes=VMEM_LIMIT)


def _row_tile(S, n_ctx):
    for tm in (512, 256, 128):
        if S % tm == 0 and n_ctx % tm == 0:
            return tm
    raise ValueError("sequence lengths must be multiples of 128")


def _dot(a, b):
    return jnp.dot(a, b, preferred_element_type=F32)


def _dot_nt(a, b):
    return lax.dot_general(a, b, (((1,), (1,)), ((), ())), preferred_element_type=F32)


def _ada_body(c_ref, w_ref, b_ref, o_ref):
    c = c_ref[...]
    s = (c * jax.nn.sigmoid(c)).astype(BF16)
    o_ref[0] = _dot(s, w_ref[0].astype(BF16)) + b_ref[0]


def _ada_mods(c_all, ada_w, ada_b):
    depth, D, N = ada_w.shape
    R = c_all.shape[0]
    tn = 1536
    return pl.pallas_call(
        _ada_body,
        grid=(depth, N // tn),
        in_specs=[pl.BlockSpec((R, D), lambda l, j: (0, 0)),
                  pl.BlockSpec((1, D, tn), lambda l, j: (l, 0, j)),
                  pl.BlockSpec((1, 1, tn), lambda l, j: (l, 0, j))],
        out_specs=pl.BlockSpec((1, R, tn), lambda l, j: (l, 0, j)),
        out_shape=jax.ShapeDtypeStruct((depth, R, N), F32),
        compiler_params=_cparams(("arbitrary", "arbitrary")),
        name="ada_mods",
    )(c_all, ada_w, ada_b.reshape(depth, 1, N))


def _norm_mod(x, g, mod, shift_idx):
    ms = jnp.mean(x * x, axis=-1, keepdims=True)
    y = x * lax.rsqrt(ms + NORM_EPS) * g
    return y * (1.0 + mod[shift_idx + 1:shift_idx + 2, :]) + mod[shift_idx:shift_idx + 1, :]


def _proj_body(x_ref, mod_ref, g_ref, w_ref, *rest, rope_chunks):
    if rope_chunks:
        cos_ref, sin_ref, o_ref = rest
    else:
        (o_ref,) = rest
    h = _norm_mod(x_ref[...], g_ref[...], mod_ref[0], 0).astype(BF16)
    tm = h.shape[0]
    n_out = o_ref.shape[1]
    if rope_chunks:
        first = (lax.broadcasted_iota(jnp.int32, (tm, LANES), 1) % 32) < 16
        cos = cos_ref[...]
        sin = sin_ref[...]
    cw = 512
    for c0 in range(0, n_out, cw):
        c1 = min(c0 + cw, n_out)
        acc = _dot(h, w_ref[:, c0:c1])
        for l0 in range(c0, c1, LANES):
            t = acc[:, l0 - c0:l0 - c0 + LANES]
            if l0 // LANES < rope_chunks:
                partner = jnp.where(first, pltpu.roll(t, LANES - 16, 1), pltpu.roll(t, 16, 1))
                t = t * cos + partner * sin
            o_ref[:, l0:l0 + LANES] = t.astype(o_ref.dtype)


def _project(x, mods, g, w, geom, out_dtype, rope=None, rope_chunks=0):
    T, D = x.shape
    N = w.shape[1]
    tm, n_lat_tiles, tiles_per_batch, B = geom
    mod_idx = lambda i: jnp.where(i < n_lat_tiles, i // tiles_per_batch, B)
    in_specs = [pl.BlockSpec((tm, D), lambda i: (i, 0)),
                pl.BlockSpec((1, N_MOD, D), lambda i: (mod_idx(i), 0, 0)),
                pl.BlockSpec((1, D), lambda i: (0, 0)),
                pl.BlockSpec((D, N), lambda i: (0, 0))]
    args = [x, mods, g.reshape(1, D), w]
    if rope_chunks:
        in_specs += [pl.BlockSpec((tm, LANES), lambda i: (i, 0))] * 2
        args += list(rope)
    return pl.pallas_call(
        functools.partial(_proj_body, rope_chunks=rope_chunks),
        grid=(T // tm,),
        in_specs=in_specs,
        out_specs=pl.BlockSpec((tm, N), lambda i: (i, 0)),
        out_shape=jax.ShapeDtypeStruct((T, N), out_dtype),
        compiler_params=_cparams(("arbitrary",)),
        name="norm_mod_proj",
    )(*args)


def _rope_tables(B, S, n_ctx):
    quarter = A_HEAD_DIM // 4
    freqs = ROPE_BASE ** (-jnp.arange(quarter, dtype=F32) / quarter)
    pos = jnp.arange(S, dtype=jnp.int32)
    ang_r = (pos // GRID_W).astype(F32)[:, None] * freqs[None, :]
    ang_c = (pos % GRID_W).astype(F32)[:, None] * freqs[None, :]
    cos = jnp.concatenate([jnp.cos(ang_r)] * 2 + [jnp.cos(ang_c)] * 2, axis=-1)
    sin = jnp.concatenate([-jnp.sin(ang_r), jnp.sin(ang_r), -jnp.sin(ang_c), jnp.sin(ang_c)], axis=-1)
    cos = jnp.tile(jnp.tile(cos, (1, 2)), (B, 1))
    sin = jnp.tile(jnp.tile(sin, (1, 2)), (B, 1))
    cos = jnp.concatenate([cos, jnp.ones((n_ctx, LANES), F32)], axis=0)
    sin = jnp.concatenate([sin, jnp.zeros((n_ctx, LANES), F32)], axis=0)
    return cos, sin


def _wattn_body(sink_ref, q_ref, *rest, n_blocks, local):
    if local:
        kp_ref, kc_ref, kn_ref, vp_ref, vc_ref, vn_ref, kx_ref, vx_ref, o_ref = rest
    else:
        kx_ref, vx_ref, o_ref = rest
    G = A_HEADS // A_KV_HEADS
    d = A_HEAD_DIM
    blk = A_BLOCK
    scale = d ** -0.5
    n_ctx = kx_ref.shape[0]
    if local:
        i = pl.program_id(1)
        qpos = lax.broadcasted_iota(jnp.int32, (G * blk, 3 * blk), 0) % blk
        koff = lax.broadcasted_iota(jnp.int32, (G * blk, 3 * blk), 1) - blk
        band = jnp.abs(koff - qpos) <= A_WINDOW
        kpos = koff + i * blk
        valid = jnp.logical_and(band, jnp.logical_and(kpos >= 0, kpos < n_blocks * blk))
    heads = [slice(h * d, (h + 1) * d) for h in range(A_KV_HEADS)]
    k_refs = (kx_ref, kp_ref, kc_ref, kn_ref) if local else (kx_ref,)
    v_refs = (vx_ref, vp_ref, vc_ref, vn_ref) if local else (vx_ref,)
    scores = []
    for h, cs in enumerate(heads):
        qh = jnp.concatenate([q_ref[:, (h * G + g) * d:(h * G + g + 1) * d] for g in range(G)], axis=0)
        k_all = jnp.concatenate([r[:, cs] for r in k_refs], axis=0)
        scores.append(_dot_nt(qh * scale, k_all))
    probs, inv_den = [], []
    for h in range(A_KV_HEADS):
        sink = jnp.concatenate([jnp.full((blk, 1), sink_ref[h * G + g], F32) for g in range(G)], axis=0)
        s = scores[h]
        if local:
            s = jnp.concatenate([s[:, :n_ctx], jnp.where(valid, s[:, n_ctx:], NEG_INF)], axis=-1)
        m = jnp.maximum(jnp.max(s, axis=-1, keepdims=True), sink)
        p = jnp.exp(s - m)
        inv_den.append(1.0 / (jnp.sum(p, axis=-1, keepdims=True) + jnp.exp(sink - m)))
        probs.append(p.astype(BF16))
    outs = [None] * A_HEADS
    for h, cs in enumerate(heads):
        v_all = jnp.concatenate([r[:, cs] for r in v_refs], axis=0)
        o = _dot(probs[h], v_all) * inv_den[h]
        for g in range(G):
            outs[h * G + g] = o[g * blk:(g + 1) * blk, :]
    o_ref[...] = jnp.concatenate(outs, axis=-1).astype(o_ref.dtype)


def _window_attention(qkv, sink, B, S, n_ctx_per):
    T = qkv.shape[0]
    blk = A_BLOCK
    qw = A_HEADS * A_HEAD_DIM
    kw = A_KV_HEADS * A_HEAD_DIM
    nb = S // blk
    kcol = qw // kw
    vcol = kcol + 1
    ctx_blk0 = (B * S) // n_ctx_per
    smem = pl.BlockSpec(memory_space=pltpu.SMEM)
    kx = pl.BlockSpec((n_ctx_per, kw), lambda b, i: (ctx_blk0 + b, kcol))
    vx = pl.BlockSpec((n_ctx_per, kw), lambda b, i: (ctx_blk0 + b, vcol))

    def nbr(col, delta):
        return pl.BlockSpec((blk, kw), lambda b, i: (b * nb + jnp.clip(i + delta, 0, nb - 1), col))

    lat = pl.pallas_call(
        functools.partial(_wattn_body, n_blocks=nb, local=True),
        grid=(B, nb),
        in_specs=[smem, pl.BlockSpec((blk, qw), lambda b, i: (b * nb + i, 0)),
                  nbr(kcol, -1), nbr(kcol, 0), nbr(kcol, 1),
                  nbr(vcol, -1), nbr(vcol, 0), nbr(vcol, 1), kx, vx],
        out_specs=pl.BlockSpec((blk, qw), lambda b, i: (b * nb + i, 0)),
        out_shape=jax.ShapeDtypeStruct((B * S, qw), BF16),
        compiler_params=_cparams(("arbitrary", "arbitrary")),
        name="window_attn",
    )(sink, qkv, qkv, qkv, qkv, qkv, qkv, qkv, qkv, qkv)
    ncb = n_ctx_per // blk
    q0 = (B * S) // blk
    ctx = pl.pallas_call(
        functools.partial(_wattn_body, n_blocks=ncb, local=False),
        grid=(B, ncb),
        in_specs=[smem, pl.BlockSpec((blk, qw), lambda b, i: (q0 + b * ncb + i, 0)), kx, vx],
        out_specs=pl.BlockSpec((blk, qw), lambda b, i: (b * ncb + i, 0)),
        out_shape=jax.ShapeDtypeStruct((B * n_ctx_per, qw), BF16),
        compiler_params=_cparams(("arbitrary", "arbitrary")),
        name="window_attn_ctx",
    )(sink, qkv, qkv, qkv)
    return jnp.concatenate([lat, ctx], axis=0)


def _nattn_body(q_ref, *rest, n_img_rows, local):
    if local:
        kp_ref, kc_ref, kn_ref, vp_ref, vc_ref, vn_ref, kx_ref, vx_ref, bias_ref, o_ref, kbuf, vbuf = rest
    else:
        kx_ref, vx_ref, o_ref = rest
    d = C_HEAD_DIM
    scale = d ** -0.5
    W = GRID_W
    n_keys = NA_WIN_R * W
    rows_per_step = q_ref.shape[0] // W

    def one_row(rr, off, dr0):
        qrow = q_ref[pl.ds(pl.multiple_of(rr * W, W), W), :]
        pairs = [slice(p * LANES, (p + 1) * LANES) for p in range(C_HEADS * d // LANES)]
        first = lax.broadcasted_iota(jnp.int32, (W, LANES), 1) < d
        zero = jnp.zeros((W, LANES), BF16)
        q_heads = []
        for ps in pairs:
            qp = qrow[:, ps] * scale
            q_heads += [jnp.where(first, qp, zero), jnp.where(first, zero, qp)]
        s_ctx = [_dot_nt(q_heads[h], kx_ref[:, pairs[h // 2]]) for h in range(C_HEADS)]
        if local:
            s_loc = [_dot_nt(q_heads[h], kbuf[pl.ds(off, n_keys), pairs[h // 2]]) for h in range(C_HEADS)]
        probs, inv_den = [], []
        for h in range(C_HEADS):
            s = s_ctx[h]
            if local:
                bias = [bias_ref[h, pl.ds(dr0 + 2 * t, 1)][0] for t in range(NA_WIN_R // 2)]
                s = jnp.concatenate([s, s_loc[h] + jnp.concatenate(bias, axis=-1)], axis=-1)
            p = jnp.exp(s - jnp.max(s, axis=-1, keepdims=True))
            inv_den.append(1.0 / jnp.sum(p, axis=-1, keepdims=True))
            probs.append(p.astype(BF16))
        n_ctx = kx_ref.shape[0]
        o_heads = []
        for h in range(C_HEADS):
            o = _dot(probs[h][:, :n_ctx], vx_ref[:, pairs[h // 2]])
            if local:
                o = o + _dot(probs[h][:, n_ctx:], vbuf[pl.ds(off, n_keys), pairs[h // 2]])
            o_heads.append(o * inv_den[h])
        outs = [jnp.where(first, o_heads[2 * p], o_heads[2 * p + 1]) for p in range(len(pairs))]
        o_ref[pl.ds(pl.multiple_of(rr * W, W), W), :] = jnp.concatenate(outs, axis=-1).astype(o_ref.dtype)

    if local:
        j = pl.program_id(1)
        blk_tokens = rows_per_step * W
        for t, (kr, vr) in enumerate(((kp_ref, vp_ref), (kc_ref, vc_ref), (kn_ref, vn_ref))):
            kbuf[t * blk_tokens:(t + 1) * blk_tokens, :] = kr[...]
            vbuf[t * blk_tokens:(t + 1) * blk_tokens, :] = vr[...]

        def body(rr, carry):
            r = j * rows_per_step + rr
            rs = jnp.clip(r - NA_WIN_R // 2, 0, n_img_rows - NA_WIN_R)
            off = pl.multiple_of((rs - (j - 1) * rows_per_step) * W, W)
            one_row(rr, off, rs - r + NA_WIN_R - 1)
            return carry

        lax.fori_loop(0, rows_per_step, body, 0)
    else:
        def body(rr, carry):
            one_row(rr, None, None)
            return carry

        lax.fori_loop(0, rows_per_step, body, 0)


def _na_bias_table(rpb):
    W = GRID_W
    qcol = jnp.arange(W)
    kcol = jnp.arange(W)
    cstart = jnp.clip(qcol - NA_WIN_C // 2, 0, W - NA_WIN_C)
    col_in = (kcol[None, :] >= cstart[:, None]) & (kcol[None, :] < cstart[:, None] + NA_WIN_C)
    dc_idx = jnp.clip(kcol[None, :] - qcol[:, None] + NA_WIN_C - 1, 0, 2 * NA_WIN_C - 2)
    t2 = rpb.astype(F32)[:, :, dc_idx]
    t2 = jnp.where(col_in[None, None], t2, NEG_INF)
    return jnp.concatenate([t2[:, :-1], t2[:, 1:]], axis=-1)


def _neighborhood_attention(qkv, rpb, B, S, n_ctx_per):
    hw = C_HEADS * C_HEAD_DIM
    R = S // GRID_W
    rows = NA_ROWS
    blk = rows * GRID_W
    nj = R // rows
    ctx_blk0 = (B * S) // n_ctx_per
    kx = pl.BlockSpec((n_ctx_per, hw), lambda b, j: (ctx_blk0 + b, 1))
    vx = pl.BlockSpec((n_ctx_per, hw), lambda b, j: (ctx_blk0 + b, 2))
    bias = _na_bias_table(rpb)

    def nbr(col, delta):
        return pl.BlockSpec((blk, hw), lambda b, j: (b * nj + jnp.clip(j + delta, 0, nj - 1), col))

    lat = pl.pallas_call(
        functools.partial(_nattn_body, n_img_rows=R, local=True),
        grid=(B, nj),
        in_specs=[pl.BlockSpec((blk, hw), lambda b, j: (b * nj + j, 0)),
                  nbr(1, -1), nbr(1, 0), nbr(1, 1), nbr(2, -1), nbr(2, 0), nbr(2, 1), kx, vx,
                  pl.BlockSpec(bias.shape, lambda b, j: (0, 0, 0, 0))],
        out_specs=pl.BlockSpec((blk, hw), lambda b, j: (b * nj + j, 0)),
        out_shape=jax.ShapeDtypeStruct((B * S, hw), BF16),
        scratch_shapes=[pltpu.VMEM((3 * blk, hw), BF16), pltpu.VMEM((3 * blk, hw), BF16)],
        compiler_params=_cparams(("arbitrary", "arbitrary")),
        name="neighborhood_attn",
    )(qkv, qkv, qkv, qkv, qkv, qkv, qkv, qkv, qkv, bias)
    q0 = (B * S) // n_ctx_per
    ctx = pl.pallas_call(
        functools.partial(_nattn_body, n_img_rows=R, local=False),
        grid=(B, 1),
        in_specs=[pl.BlockSpec((n_ctx_per, hw), lambda b, j: (q0 + b, 0)), kx, vx],
        out_specs=pl.BlockSpec((n_ctx_per, hw), lambda b, j: (b, 0)),
        out_shape=jax.ShapeDtypeStruct((B * n_ctx_per, hw), BF16),
        compiler_params=_cparams(("arbitrary", "arbitrary")),
        name="neighborhood_attn_ctx",
    )(qkv, qkv, qkv)
    return jnp.concatenate([lat, ctx], axis=0)


def _log_sigmoid(x):
    return jnp.minimum(x, 0.0) - jnp.log1p(jnp.exp(-jnp.abs(x)))


def _mlstm_body(qf, kf, vf, gcf, grf, qb, kb, vb, gcb, grb, bgr_ref, bgc_ref, hf_ref, hb_ref, st_ref, m_ref):
    ch = qf.shape[0]
    dk, dv = M_QK_DIM, M_V_DIM
    H = M_HEADS

    @pl.when(pl.program_id(1) == 0)
    def _():
        st_ref[...] = jnp.zeros_like(st_ref)
        m_ref[...] = jnp.full_like(m_ref, NEG_INF)

    row_i = lax.broadcasted_iota(jnp.int32, (ch, ch), 0)
    col_i = lax.broadcasted_iota(jnp.int32, (ch, ch), 1)
    lower = row_i >= col_i
    upper = row_i <= col_i
    tril = lower.astype(F32)
    triu = upper.astype(F32)
    hi = lax.Precision.HIGHEST
    dirs = ((qf, kf, vf, gcf, grf, hf_ref), (qb, kb, vb, gcb, grb, hb_ref))
    chains = []
    for direction, (q_ref, k_ref, v_ref, gc_ref, gr_ref, o_ref) in enumerate(dirs):
        fwd = direction == 0
        gcol = gc_ref[...] + bgr_ref[...]
        grow = gr_ref[...] + bgc_ref[...]
        cum_c = jnp.dot(tril if fwd else triu, _log_sigmoid(gcol), precision=hi, preferred_element_type=F32)
        cum_r = jnp.dot(_log_sigmoid(grow), triu if fwd else tril, precision=hi, preferred_element_type=F32)
        for h in range(H):
            c = direction * H + h
            q = q_ref[:, h * dk:(h + 1) * dk].astype(BF16)
            k_s = k_ref[:, h * dk:(h + 1) * dk] * (dk ** -0.5)
            state = st_ref[c]
            chains.append(dict(c=c, h=h, fwd=fwd, gcol=gcol, grow=grow, cum_c=cum_c, cum_r=cum_r, state=state,
                               k_t=k_s.T.astype(BF16), v=v_ref[:, h * dv:(h + 1) * dv], o_ref=o_ref,
                               s_raw=_dot_nt(q, k_s.astype(BF16)), q_state=_dot(q, state.astype(BF16))))
    for ck in chains:
        h, fwd = ck["h"], ck["fwd"]
        gi = (0 if fwd else 2 * H) + h
        gf = gi + H
        b_col = ck["cum_c"][:, gf:gf + 1]
        b_row = ck["cum_r"][gf:gf + 1, :]
        i_col = ck["gcol"][:, gi:gi + 1]
        i_row = ck["grow"][gi:gi + 1, :]
        b_last = b_row[:, ch - 1:ch] if fwd else b_row[:, 0:1]
        m_prev = m_ref[ck["c"]][:, 0:1]
        log_d = jnp.where(lower if fwd else upper, b_col - b_row + i_row, NEG_INF)
        m_inter = b_col + m_prev
        m_t = jnp.maximum(jnp.max(log_d, axis=-1, keepdims=True), m_inter)
        s = ck["s_raw"] * jnp.exp(log_d - m_t)
        m_new = jnp.maximum(b_last + m_prev, jnp.max(b_last - b_row + i_row, axis=-1, keepdims=True))
        w_col = jnp.exp(b_last - b_col + i_col - m_new)
        v = ck["v"]
        wv = jnp.concatenate([v * w_col, jnp.broadcast_to(w_col, (ch, LANES))], axis=-1).astype(BF16)
        ck.update(m_t=m_t, inter=jnp.exp(m_inter - m_t), m_new=m_new, decay=jnp.exp(b_last + m_prev - m_new),
                  s_sum=jnp.sum(s, axis=-1, keepdims=True), sv=_dot(s.astype(BF16), v.astype(BF16)),
                  upd=_dot(ck["k_t"], wv))
    for ck in chains:
        h, c = ck["h"], ck["c"]
        num = ck["sv"] + ck["inter"] * ck["q_state"][:, :dv]
        den = ck["s_sum"] + ck["inter"] * ck["q_state"][:, dv:dv + 1]
        ck["o_ref"][:, h * dv:(h + 1) * dv] = num * (1.0 / jnp.maximum(jnp.abs(den), jnp.exp(-ck["m_t"])))
        st_ref[c] = ck["decay"] * ck["state"] + ck["upd"]
        m_ref[c] = jnp.broadcast_to(ck["m_new"], (1, LANES))


def _mlstm_scan(p, gates_t, b_gate, B, S, n_ctx_per):
    T = p.shape[0]
    ch = MLSTM_CHUNK
    H = M_HEADS
    vw = H * M_V_DIM
    ncc = n_ctx_per // ch
    nlc = S // ch
    gate_col = (2 * H * M_QK_DIM + 2 * vw) // LANES

    def fwd_idx(b, s):
        return jnp.where(s < ncc, (B * S + b * n_ctx_per) // ch + s, b * nlc + s - ncc)

    def bwd_idx(b, s):
        return jnp.where(s < ncc, (B * S + b * n_ctx_per) // ch + ncc - 1 - s, b * nlc + nlc - 1 - (s - ncc))

    def specs(idx):
        return [pl.BlockSpec((ch, H * M_QK_DIM), lambda b, s: (idx(b, s), 0)),
                pl.BlockSpec((ch, H * M_QK_DIM), lambda b, s: (idx(b, s), 1)),
                pl.BlockSpec((ch, vw), lambda b, s: (idx(b, s), 1)),
                pl.BlockSpec((ch, LANES), lambda b, s: (idx(b, s), gate_col)),
                pl.BlockSpec((4 * H, ch), lambda b, s: (0, idx(b, s)))]

    bg_row = jnp.zeros((1, LANES), F32).at[0, :4 * H].set(b_gate.astype(F32))
    bg_col = b_gate.astype(F32).reshape(4 * H, 1)
    return pl.pallas_call(
        _mlstm_body,
        grid=(B, ncc + nlc),
        in_specs=specs(fwd_idx) + specs(bwd_idx) + [pl.BlockSpec((1, LANES), lambda b, s: (0, 0)),
                                                     pl.BlockSpec((4 * H, 1), lambda b, s: (0, 0))],
        out_specs=[pl.BlockSpec((ch, vw), lambda b, s: (fwd_idx(b, s), 0)),
                   pl.BlockSpec((ch, vw), lambda b, s: (bwd_idx(b, s), 0))],
        out_shape=[jax.ShapeDtypeStruct((T, vw), F32)] * 2,
        scratch_shapes=[pltpu.VMEM((2 * H, M_QK_DIM, M_V_DIM + LANES), F32),
                        pltpu.VMEM((2 * H, 1, LANES), F32)],
        compiler_params=_cparams(("arbitrary", "arbitrary")),
        name="mlstm_scan",
    )(p, p, p, p, gates_t, p, p, p, p, gates_t, bg_row, bg_col)


def _outproj_body(*refs, mlstm):
    if mlstm:
        hf_ref, hb_ref, og_ref, hn_ref, x_ref, mod_ref, g2_ref, wo_ref, rw_ref, rb_ref, xo_ref, h2_ref, lg_ref = refs
        dv = M_V_DIM
        hsum = hf_ref[...] + hb_ref[...]
        parts = []
        for h in range(M_HEADS):
            t = hsum[:, h * dv:(h + 1) * dv]
            parts.append(t * lax.rsqrt(jnp.mean(t * t, axis=-1, keepdims=True) + NORM_EPS))
        a = (jnp.concatenate(parts, axis=-1) * hn_ref[...] * jax.nn.sigmoid(og_ref[...])).astype(BF16)
    else:
        a_ref, x_ref, mod_ref, g2_ref, wo_ref, rw_ref, rb_ref, xo_ref, h2_ref, lg_ref = refs
        a = a_ref[...]
    mod = mod_ref[0]
    x = x_ref[...] + mod[2:3, :] * _dot(a, wo_ref[...])
    xo_ref[...] = x
    h2 = _norm_mod(x, g2_ref[...], mod, 3)
    h2_hi = h2.astype(BF16)
    h2_ref[...] = h2_hi
    h2_lo = (h2 - h2_hi.astype(F32)).astype(BF16)
    t = _dot(h2_hi, rw_ref[...])
    lg_ref[...] = t[:, :LANES] + (t[:, LANES:] + _dot(h2_lo, rw_ref[:, :LANES])) + rb_ref[...]


def _out_project(mixer_out, x, mods, g2, wo, router_w, router_b, geom, mlstm):
    T, D = x.shape
    tm, n_lat_tiles, tiles_per_batch, B = geom
    mod_idx = lambda i: jnp.where(i < n_lat_tiles, i // tiles_per_batch, B)
    row = pl.BlockSpec((tm, D), lambda i: (i, 0))
    vec = pl.BlockSpec((1, D), lambda i: (0, 0))
    rw = jnp.zeros((D, LANES), F32).at[:, :N_EXPERTS].set(router_w.astype(F32))
    rw_hi = rw.astype(BF16)
    rw = jnp.concatenate([rw_hi, (rw - rw_hi.astype(F32)).astype(BF16)], axis=1)
    rb = jnp.zeros((1, LANES), F32).at[0, :N_EXPERTS].set(router_b.astype(F32))
    if mlstm:
        hf, hb, p, hnorm = mixer_out
        lead_specs = [row, row, pl.BlockSpec((tm, D), lambda i: (i, 2)), vec]
        lead = [hf, hb, p, hnorm.reshape(1, D)]
    else:
        lead_specs = [row]
        lead = [mixer_out]
    return pl.pallas_call(
        functools.partial(_outproj_body, mlstm=mlstm),
        grid=(T // tm,),
        in_specs=lead_specs + [row, pl.BlockSpec((1, N_MOD, D), lambda i: (mod_idx(i), 0, 0)), vec,
                               pl.BlockSpec((D, D), lambda i: (0, 0)),
                               pl.BlockSpec((D, 2 * LANES), lambda i: (0, 0)),
                               pl.BlockSpec((1, LANES), lambda i: (0, 0))],
        out_specs=[row, row, pl.BlockSpec((tm, LANES), lambda i: (i, 0))],
        out_shape=[jax.ShapeDtypeStruct((T, D), F32), jax.ShapeDtypeStruct((T, D), BF16),
                   jax.ShapeDtypeStruct((T, LANES), F32)],
        compiler_params=_cparams(("arbitrary",)),
        name="out_proj_norm_router",
    )(*lead, x, mods, g2.reshape(1, D), wo, rw, rb)


def _moe_body(be_ref, nused_ref, x_ref, win_ref, wout_ref, bin_ref, bout_ref, o_ref, win_b, wout_b, stage, act_scr):
    i = pl.program_id(0)
    D, F2 = win_b.shape
    F = F2 // 2
    rows = x_ref.shape[0]
    used = i < nused_ref[0]
    changed = jnp.logical_or(i == 0, be_ref[i] != be_ref[jnp.maximum(i - 1, 0)])

    @pl.when(jnp.logical_and(used, changed))
    def _():
        def cast_rows(r, carry):
            rs = pl.ds(pl.multiple_of(r * LANES, LANES), LANES)
            win_b[rs, :] = win_ref[0, rs, :].astype(BF16)
            return carry

        lax.fori_loop(0, D // LANES, cast_rows, 0)
        for c in range(D // LANES):
            cs = slice(c * LANES, (c + 1) * LANES)
            stage[c, pl.ds(0, F // 2, stride=2), :] = wout_ref[0, :F // 2, cs]
            stage[c, pl.ds(1, F // 2, stride=2), :] = wout_ref[0, F // 2:, cs]
            wout_b[:, cs] = stage[c].astype(BF16)

    @pl.when(used)
    def _():
        x = x_ref[...]
        even = (lax.broadcasted_iota(jnp.int32, (rows, LANES), 1) % 2) == 0
        cw = 256
        for c0 in range(0, F, cw):
            ha = _dot(x, win_b[:, c0:c0 + cw]) + bin_ref[0, :, c0:c0 + cw]
            hb = _dot(x, win_b[:, F + c0:F + c0 + cw]) + bin_ref[0, :, F + c0:F + c0 + cw]
            for l0 in range(0, cw, LANES):
                a = ha[:, l0:l0 + LANES]
                b = hb[:, l0:l0 + LANES]
                gate = jnp.minimum(jnp.where(even, a, pltpu.roll(b, 1, 1)), SWIGLU_LIMIT)
                up = jnp.clip(jnp.where(even, pltpu.roll(a, LANES - 1, 1), b), -SWIGLU_LIMIT, SWIGLU_LIMIT)
                act = (up + 1.0) * (gate * jax.nn.sigmoid(SWIGLU_ALPHA * gate))
                act_scr[:, c0 + l0:c0 + l0 + LANES] = act.astype(BF16)
        o_ref[...] = (_dot(act_scr[...], wout_b[...]) + bout_ref[0]).astype(o_ref.dtype)

    @pl.when(jnp.logical_not(used))
    def _():
        o_ref[...] = jnp.zeros_like(o_ref)


def _moe_experts(xb, block_expert, n_used, layer, w_in, b_in, w_out, b_out):
    n_rows, D = xb.shape
    n_blocks = n_rows // MOE_BLOCK
    L, E, _, F2 = w_in.shape
    F = F2 // 2
    wspec = lambda shape: pl.BlockSpec((1,) + shape, lambda i, be, nu: (layer * E + be[i], 0, 0))
    grid_spec = pltpu.PrefetchScalarGridSpec(
        num_scalar_prefetch=2,
        grid=(n_blocks,),
        in_specs=[pl.BlockSpec((MOE_BLOCK, D), lambda i, be, nu: (i, 0)),
                  wspec((D, F2)), wspec((F, D)), wspec((1, F2)), wspec((1, D))],
        out_specs=pl.BlockSpec((MOE_BLOCK, D), lambda i, be, nu: (i, 0)),
        scratch_shapes=[pltpu.VMEM((D, F2), BF16), pltpu.VMEM((F, D), BF16),
                        pltpu.VMEM((D // LANES, F, LANES), F32), pltpu.VMEM((MOE_BLOCK, F), BF16)],
    )
    return pl.pallas_call(
        _moe_body,
        grid_spec=grid_spec,
        out_shape=jax.ShapeDtypeStruct((n_rows, D), F32),
        compiler_params=_cparams(("arbitrary",)),
        name="moe_experts",
    )(block_expert, n_used, xb, w_in.reshape(L * E, D, F2), w_out.reshape(L * E, F, D),
      b_in.reshape(L * E, 1, F2), b_out.reshape(L * E, 1, D))


COMBINE_TILE = 256
COMBINE_UNROLL = 8


def _combine_body(idx_ref, idx_next_ref, gate_ref, x_ref, mod_ref, yb_hbm, o_ref, buf, sem):
    i = pl.program_id(0)
    n = pl.num_programs(0)
    tm = x_ref.shape[0]
    n_rows = TOP_K * tm
    slot = i % 2

    def fetch_rows(rows_ref, into):
        def body(j, carry):
            for u in range(COMBINE_UNROLL):
                r = j * COMBINE_UNROLL + u
                pltpu.make_async_copy(yb_hbm.at[pl.ds(rows_ref[r], 1)], buf.at[into, pl.ds(r, 1)],
                                      sem.at[into]).start(priority=u % 2)
            return carry

        lax.fori_loop(0, n_rows // COMBINE_UNROLL, body, 0)

    @pl.when(i == 0)
    def _():
        fetch_rows(idx_ref, 0)

    @pl.when(i + 1 < n)
    def _():
        fetch_rows(idx_next_ref, 1 - slot)

    pltpu.make_async_copy(yb_hbm.at[pl.ds(0, n_rows)], buf.at[slot], sem.at[slot]).wait()
    gates = gate_ref[...]
    y = gates[:, 0:1] * buf[slot, 0:tm, :]
    for k in range(1, TOP_K):
        y = y + gates[:, k:k + 1] * buf[slot, k * tm:(k + 1) * tm, :]
    o_ref[...] = x_ref[...] + mod_ref[0][5:6, :] * y


def _moe_combine(yb, dest, gates, x, mods, geom):
    T, D = x.shape
    _, n_lat_tiles, tiles_per_batch, B = geom
    tm = COMBINE_TILE
    per = geom[0] // tm
    mod_idx = lambda i: jnp.where(i // per < n_lat_tiles, (i // per) // tiles_per_batch, B)
    n_tiles = T // tm
    rows = dest.reshape(n_tiles, tm, TOP_K).transpose(0, 2, 1).reshape(-1).astype(jnp.int32)
    nr = TOP_K * tm
    return pl.pallas_call(
        _combine_body,
        grid=(n_tiles,),
        in_specs=[pl.BlockSpec((nr,), lambda i: (i,), memory_space=pltpu.SMEM),
                  pl.BlockSpec((nr,), lambda i: (jnp.minimum(i + 1, n_tiles - 1),), memory_space=pltpu.SMEM),
                  pl.BlockSpec((tm, TOP_K), lambda i: (i, 0)),
                  pl.BlockSpec((tm, D), lambda i: (i, 0)),
                  pl.BlockSpec((1, N_MOD, D), lambda i: (mod_idx(i), 0, 0)),
                  pl.BlockSpec(memory_space=pl.ANY)],
        out_specs=pl.BlockSpec((tm, D), lambda i: (i, 0)),
        out_shape=jax.ShapeDtypeStruct((T, D), F32),
        scratch_shapes=[pltpu.VMEM((2, nr, D), F32), pltpu.SemaphoreType.DMA((2,))],
        compiler_params=_cparams(("arbitrary",)),
        name="moe_combine",
    )(rows, rows, gates, x, mods, yb)


def _moe_layer(xs, h2, logits, mods, geom, layer, w_in, b_in, w_out, b_out):
    T, D = h2.shape
    top_val, top_idx = lax.top_k(logits[:, :N_EXPERTS], TOP_K)
    gates = jax.nn.softmax(top_val, axis=-1)
    onehot = jnp.sum(jax.nn.one_hot(top_idx, N_EXPERTS, dtype=F32), axis=1)
    seg = LANES
    oh = onehot.reshape(T // seg, seg, N_EXPERTS)
    strict_lower = jnp.tril(jnp.ones((seg, seg), F32), -1)
    within = jnp.einsum('ij,sje->sie', strict_lower, oh, precision=lax.Precision.HIGHEST)
    seg_tot = jnp.sum(oh, axis=1)
    seg_off = jnp.cumsum(seg_tot, axis=0) - seg_tot
    before = (within + seg_off[:, None, :]).reshape(T, N_EXPERTS).astype(jnp.int32)
    counts = jnp.sum(seg_tot, axis=0).astype(jnp.int32)
    starts = jnp.cumsum(counts) - counts
    padded = (counts + MOE_BLOCK - 1) // MOE_BLOCK * MOE_BLOCK
    pad_ends = jnp.cumsum(padded)
    pad_starts = pad_ends - padded
    dest = pad_starts[top_idx] + jnp.take_along_axis(before, top_idx, axis=1)
    n_assign = T * TOP_K
    n_blocks = -(-n_assign // MOE_BLOCK) + N_EXPERTS
    n_rows = n_blocks * MOE_BLOCK
    block_start = jnp.arange(n_blocks, dtype=jnp.int32) * MOE_BLOCK
    block_expert = jnp.minimum(jnp.sum((pad_ends[None, :] <= block_start[:, None]).astype(jnp.int32), axis=1),
                               N_EXPERTS - 1)
    n_used = (pad_ends[-1] // MOE_BLOCK).astype(jnp.int32).reshape(1)
    order = jnp.argsort(top_idx.reshape(-1), stable=True).astype(jnp.int32)
    row_e = jnp.repeat(block_expert, MOE_BLOCK)
    row_id = jnp.arange(n_rows, dtype=jnp.int32)
    off = row_id - pad_starts[row_e].astype(jnp.int32)
    src_sorted = jnp.clip(starts[row_e].astype(jnp.int32) + off, 0, n_assign - 1)
    src_tok = jnp.where(off < counts[row_e], order[src_sorted] // TOP_K, row_id % T)
    xb = h2.at[src_tok].get(mode='promise_in_bounds')
    yb = _moe_experts(xb, block_expert, n_used, layer, w_in, b_in, w_out, b_out)
    return _moe_combine(yb, dest, gates, xs, mods, geom)


def _final_norm_body(x_ref, g_ref, o_ref):
    x = x_ref[...]
    o_ref[...] = x * lax.rsqrt(jnp.mean(x * x, axis=-1, keepdims=True) + NORM_EPS) * g_ref[...]


def _final_norm(x, g, n_rows, tm):
    D = x.shape[1]
    return pl.pallas_call(
        _final_norm_body,
        grid=(n_rows // tm,),
        in_specs=[pl.BlockSpec((tm, D), lambda i: (i, 0)), pl.BlockSpec((1, D), lambda i: (0, 0))],
        out_specs=pl.BlockSpec((tm, D), lambda i: (i, 0)),
        out_shape=jax.ShapeDtypeStruct((n_rows, D), F32),
        compiler_params=_cparams(("arbitrary",)),
        name="final_norm",
    )(x, g.reshape(1, D))


def kernel(x, c, ctx, c_ctx, ada_w, ada_b, norm1_g, norm2_g, a_wqkv, a_wo, a_sink, m_win, m_bgate, m_hnorm,
           m_wo, n_wqkv, n_wo, n_rpb, router_w, router_b, exp_w_in, exp_b_in, exp_w_out, exp_b_out, final_g):
    B, S, D = x.shape
    Lc = ctx.shape[1]
    depth = ada_w.shape[0]
    n_lat, n_ctx = B * S, B * Lc
    assert D == D_MODEL and S % (NA_ROWS * GRID_W) == 0 and Lc % MLSTM_CHUNK == 0 and n_lat % Lc == 0
    tm = _row_tile(S, n_ctx)
    geom = (tm, n_lat // tm, S // tm, B)

    xs = jnp.concatenate([x.reshape(n_lat, D), ctx.reshape(n_ctx, D)], axis=0).astype(F32)
    c_rows = -(-(B + 1) // 8) * 8
    c_all = jnp.zeros((c_rows, D), F32).at[:B].set(c.astype(F32)).at[B].set(c_ctx.astype(F32))
    mods = _ada_mods(c_all, ada_w.astype(F32), ada_b.astype(F32)).reshape(depth, c_rows, N_MOD, D)
    rope = _rope_tables(B, S, n_ctx)

    for layer in range(depth):
        kind = layer % N_MIXERS
        j = layer // N_MIXERS
        mod_l = mods[layer]
        g1 = norm1_g[layer].astype(F32)
        if kind == 0:
            rope_cols = (A_HEADS + A_KV_HEADS) * A_HEAD_DIM
            qkv = _project(xs, mod_l, g1, a_wqkv[j].astype(BF16), geom, BF16, rope, rope_cols // LANES)
            mixed = _window_attention(qkv, a_sink[j].astype(F32), B, S, Lc)
            wo = a_wo[j]
        elif kind == 1:
            n_in = m_win.shape[2]
            n_pad = -(-n_in // LANES) * LANES
            w_in = jnp.zeros((D, n_pad), BF16).at[:, :n_in].set(m_win[j].astype(BF16))
            p = _project(xs, mod_l, g1, w_in, geom, F32)
            gates_t = p[:, n_in - 4 * M_HEADS:n_in].T
            hf, hb = _mlstm_scan(p, gates_t, m_bgate[j], B, S, Lc)
            mixed = (hf, hb, p, m_hnorm[j].astype(F32))
            wo = m_wo[j]
        else:
            qkv = _project(xs, mod_l, g1, n_wqkv[j].astype(BF16), geom, BF16)
            mixed = _neighborhood_attention(qkv, n_rpb[j], B, S, Lc)
            wo = n_wo[j]
        xs, h2, logits = _out_project(mixed, xs, mod_l, norm2_g[layer].astype(F32), wo.astype(BF16),
                                      router_w[layer], router_b[layer], geom, kind == 1)
        xs = _moe_layer(xs, h2, logits, mod_l, geom, layer, exp_w_in, exp_b_in, exp_w_out, exp_b_out)
    return _final_norm(xs, final_g.astype(F32), n_lat, tm).reshape(B, S, D).astype(x.dtype)
```

```python
import functools

import jax
import jax.numpy as jnp
from jax import lax
from jax.experimental import pallas as pl
from jax.experimental.pallas import tpu as pltpu

F32 = jnp.float32
BF16 = jnp.bfloat16

D_MODEL = 1024
GRID_W = 64
N_MIXERS = 3
N_MOD = 6
NORM_EPS = 1e-6
NEG_INF = -1e30
A_HEADS = 16
A_KV_HEADS = 4
A_HEAD_DIM = D_MODEL // A_HEADS
A_WINDOW = 128
A_BLOCK = 128
ROPE_BASE = 10000.0
M_HEADS = 4
M_V_DIM = D_MODEL // M_HEADS
M_QK_DIM = M_V_DIM // 2
C_HEADS = 16
C_HEAD_DIM = D_MODEL // C_HEADS
NA_WIN_R = 8
NA_WIN_C = 16
N_EXPERTS = 32
TOP_K = 4
D_FF = D_MODEL
SWIGLU_LIMIT = 7.0
SWIGLU_ALPHA = 1.702
MOE_BLOCK = 512

LANES = 128
MLSTM_CHUNK = 256
NA_ROWS = 8
VMEM_LIMIT = 56 * 1024 * 1024


def _cparams(sem):
    return pltpu.CompilerParams(dimension_semantics=sem, vmem_limit_bytes=VMEM_LIMIT)


def _row_tile(S, n_ctx):
    for tm in (512, 256, 128):
        if S % tm == 0 and n_ctx % tm == 0:
            return tm
    raise ValueError("sequence lengths must be multiples of 128")


def _dot(a, b):
    return jnp.dot(a, b, preferred_element_type=F32)


def _dot_nt(a, b):
    return lax.dot_general(a, b, (((1,), (1,)), ((), ())), preferred_element_type=F32)


def _ada_body(c_ref, w_ref, b_ref, o_ref):
    c = c_ref[...]
    s = (c * jax.nn.sigmoid(c)).astype(BF16)
    o_ref[0] = _dot(s, w_ref[0].astype(BF16)) + b_ref[0]


def _ada_mods(c_all, ada_w, ada_b):
    depth, D, N = ada_w.shape
    R = c_all.shape[0]
    tn = 1536
    return pl.pallas_call(
        _ada_body,
        grid=(depth, N // tn),
        in_specs=[pl.BlockSpec((R, D), lambda l, j: (0, 0)),
                  pl.BlockSpec((1, D, tn), lambda l, j: (l, 0, j)),
                  pl.BlockSpec((1, 1, tn), lambda l, j: (l, 0, j))],
        out_specs=pl.BlockSpec((1, R, tn), lambda l, j: (l, 0, j)),
        out_shape=jax.ShapeDtypeStruct((depth, R, N), F32),
        compiler_params=_cparams(("arbitrary", "arbitrary")),
        name="ada_mods",
    )(c_all, ada_w, ada_b.reshape(depth, 1, N))


def _norm_mod(x, g, mod, shift_idx):
    ms = jnp.mean(x * x, axis=-1, keepdims=True)
    y = x * lax.rsqrt(ms + NORM_EPS) * g
    return y * (1.0 + mod[shift_idx + 1:shift_idx + 2, :]) + mod[shift_idx:shift_idx + 1, :]


def _proj_body(x_ref, mod_ref, g_ref, w_ref, *rest, rope_chunks):
    if rope_chunks:
        cos_ref, sin_ref, o_ref = rest
    else:
        (o_ref,) = rest
    h = _norm_mod(x_ref[...], g_ref[...], mod_ref[0], 0).astype(BF16)
    tm = h.shape[0]
    n_out = o_ref.shape[1]
    if rope_chunks:
        first = (lax.broadcasted_iota(jnp.int32, (tm, LANES), 1) % 32) < 16
        cos = cos_ref[...]
        sin = sin_ref[...]
    cw = 512
    for c0 in range(0, n_out, cw):
        c1 = min(c0 + cw, n_out)
        acc = _dot(h, w_ref[:, c0:c1])
        for l0 in range(c0, c1, LANES):
            t = acc[:, l0 - c0:l0 - c0 + LANES]
            if l0 // LANES < rope_chunks:
                partner = jnp.where(first, pltpu.roll(t, LANES - 16, 1), pltpu.roll(t, 16, 1))
                t = t * cos + partner * sin
            o_ref[:, l0:l0 + LANES] = t.astype(o_ref.dtype)


def _project(x, mods, g, w, geom, out_dtype, rope=None, rope_chunks=0):
    T, D = x.shape
    N = w.shape[1]
    tm, n_lat_tiles, tiles_per_batch, B = geom
    mod_idx = lambda i: jnp.where(i < n_lat_tiles, i // tiles_per_batch, B)
    in_specs = [pl.BlockSpec((tm, D), lambda i: (i, 0)),
                pl.BlockSpec((1, N_MOD, D), lambda i: (mod_idx(i), 0, 0)),
                pl.BlockSpec((1, D), lambda i: (0, 0)),
                pl.BlockSpec((D, N), lambda i: (0, 0))]
    args = [x, mods, g.reshape(1, D), w]
    if rope_chunks:
        in_specs += [pl.BlockSpec((tm, LANES), lambda i: (i, 0))] * 2
        args += list(rope)
    return pl.pallas_call(
        functools.partial(_proj_body, rope_chunks=rope_chunks),
        grid=(T // tm,),
        in_specs=in_specs,
        out_specs=pl.BlockSpec((tm, N), lambda i: (i, 0)),
        out_shape=jax.ShapeDtypeStruct((T, N), out_dtype),
        compiler_params=_cparams(("arbitrary",)),
        name="norm_mod_proj",
    )(*args)


def _rope_tables(B, S, n_ctx):
    quarter = A_HEAD_DIM // 4
    freqs = ROPE_BASE ** (-jnp.arange(quarter, dtype=F32) / quarter)
    pos = jnp.arange(S, dtype=jnp.int32)
    ang_r = (pos // GRID_W).astype(F32)[:, None] * freqs[None, :]
    ang_c = (pos % GRID_W).astype(F32)[:, None] * freqs[None, :]
    cos = jnp.concatenate([jnp.cos(ang_r)] * 2 + [jnp.cos(ang_c)] * 2, axis=-1)
    sin = jnp.concatenate([-jnp.sin(ang_r), jnp.sin(ang_r), -jnp.sin(ang_c), jnp.sin(ang_c)], axis=-1)
    cos = jnp.tile(jnp.tile(cos, (1, 2)), (B, 1))
    sin = jnp.tile(jnp.tile(sin, (1, 2)), (B, 1))
    cos = jnp.concatenate([cos, jnp.ones((n_ctx, LANES), F32)], axis=0)
    sin = jnp.concatenate([sin, jnp.zeros((n_ctx, LANES), F32)], axis=0)
    return cos, sin


def _wattn_body(sink_ref, q_ref, *rest, n_blocks, local):
    if local:
        kp_ref, kc_ref, kn_ref, vp_ref, vc_ref, vn_ref, kx_ref, vx_ref, o_ref = rest
    else:
        kx_ref, vx_ref, o_ref = rest
    G = A_HEADS // A_KV_HEADS
    d = A_HEAD_DIM
    blk = A_BLOCK
    scale = d ** -0.5
    n_ctx = kx_ref.shape[0]
    if local:
        i = pl.program_id(1)
        qpos = lax.broadcasted_iota(jnp.int32, (G * blk, 3 * blk), 0) % blk
        koff = lax.broadcasted_iota(jnp.int32, (G * blk, 3 * blk), 1) - blk
        band = jnp.abs(koff - qpos) <= A_WINDOW
        kpos = koff + i * blk
        valid = jnp.logical_and(band, jnp.logical_and(kpos >= 0, kpos < n_blocks * blk))
    heads = [slice(h * d, (h + 1) * d) for h in range(A_KV_HEADS)]
    k_refs = (kx_ref, kp_ref, kc_ref, kn_ref) if local else (kx_ref,)
    v_refs = (vx_ref, vp_ref, vc_ref, vn_ref) if local else (vx_ref,)
    scores = []
    for h, cs in enumerate(heads):
        qh = jnp.concatenate([q_ref[:, (h * G + g) * d:(h * G + g + 1) * d] for g in range(G)], axis=0)
        k_all = jnp.concatenate([r[:, cs] for r in k_refs], axis=0)
        scores.append(_dot_nt(qh * scale, k_all))
    probs, inv_den = [], []
    for h in range(A_KV_HEADS):
        sink = jnp.concatenate([jnp.full((blk, 1), sink_ref[h * G + g], F32) for g in range(G)], axis=0)
        s = scores[h]
        if local:
            s = jnp.concatenate([s[:, :n_ctx], jnp.where(valid, s[:, n_ctx:], NEG_INF)], axis=-1)
        m = jnp.maximum(jnp.max(s, axis=-1, keepdims=True), sink)
        p = jnp.exp(s - m)
        inv_den.append(1.0 / (jnp.sum(p, axis=-1, keepdims=True) + jnp.exp(sink - m)))
        probs.append(p.astype(BF16))
    outs = [None] * A_HEADS
    for h, cs in enumerate(heads):
        v_all = jnp.concatenate([r[:, cs] for r in v_refs], axis=0)
        o = _dot(probs[h], v_all) * inv_den[h]
        for g in range(G):
            outs[h * G + g] = o[g * blk:(g + 1) * blk, :]
    o_ref[...] = jnp.concatenate(outs, axis=-1).astype(o_ref.dtype)


def _window_attention(qkv, sink, B, S, n_ctx_per):
    T = qkv.shape[0]
    blk = A_BLOCK
    qw = A_HEADS * A_HEAD_DIM
    kw = A_KV_HEADS * A_HEAD_DIM
    nb = S // blk
    kcol = qw // kw
    vcol = kcol + 1
    ctx_blk0 = (B * S) // n_ctx_per
    smem = pl.BlockSpec(memory_space=pltpu.SMEM)
    kx = pl.BlockSpec((n_ctx_per, kw), lambda b, i: (ctx_blk0 + b, kcol))
    vx = pl.BlockSpec((n_ctx_per, kw), lambda b, i: (ctx_blk0 + b, vcol))

    def nbr(col, delta):
        return pl.BlockSpec((blk, kw), lambda b, i: (b * nb + jnp.clip(i + delta, 0, nb - 1), col))

    lat = pl.pallas_call(
        functools.partial(_wattn_body, n_blocks=nb, local=True),
        grid=(B, nb),
        in_specs=[smem, pl.BlockSpec((blk, qw), lambda b, i: (b * nb + i, 0)),
                  nbr(kcol, -1), nbr(kcol, 0), nbr(kcol, 1),
                  nbr(vcol, -1), nbr(vcol, 0), nbr(vcol, 1), kx, vx],
        out_specs=pl.BlockSpec((blk, qw), lambda b, i: (b * nb + i, 0)),
        out_shape=jax.ShapeDtypeStruct((B * S, qw), BF16),
        compiler_params=_cparams(("arbitrary", "arbitrary")),
        name="window_attn",
    )(sink, qkv, qkv, qkv, qkv, qkv, qkv, qkv, qkv, qkv)
    ncb = n_ctx_per // blk
    q0 = (B * S) // blk
    ctx = pl.pallas_call(
        functools.partial(_wattn_body, n_blocks=ncb, local=False),
        grid=(B, ncb),
        in_specs=[smem, pl.BlockSpec((blk, qw), lambda b, i: (q0 + b * ncb + i, 0)), kx, vx],
        out_specs=pl.BlockSpec((blk, qw), lambda b, i: (b * ncb + i, 0)),
        out_shape=jax.ShapeDtypeStruct((B * n_ctx_per, qw), BF16),
        compiler_params=_cparams(("arbitrary", "arbitrary")),
        name="window_attn_ctx",
    )(sink, qkv, qkv, qkv)
    return lat, ctx


def _nattn_body(q_ref, *rest, n_img_rows, local):
    if local:
        kp_ref, kc_ref, kn_ref, vp_ref, vc_ref, vn_ref, kx_ref, vx_ref, bias_ref, o_ref, kbuf, vbuf = rest
    else:
        kx_ref, vx_ref, o_ref = rest
    d = C_HEAD_DIM
    scale = d ** -0.5
    W = GRID_W
    n_keys = NA_WIN_R * W
    rows_per_step = q_ref.shape[0] // W

    def one_row(rr, off, dr0):
        qrow = q_ref[pl.ds(pl.multiple_of(rr * W, W), W), :]
        pairs = [slice(p * LANES, (p + 1) * LANES) for p in range(C_HEADS * d // LANES)]
        first = lax.broadcasted_iota(jnp.int32, (W, LANES), 1) < d
        zero = jnp.zeros((W, LANES), BF16)
        q_heads = []
        for ps in pairs:
            qp = qrow[:, ps] * scale
            q_heads += [jnp.where(first, qp, zero), jnp.where(first, zero, qp)]
        s_ctx = [_dot_nt(q_heads[h], kx_ref[:, pairs[h // 2]]) for h in range(C_HEADS)]
        if local:
            s_loc = [_dot_nt(q_heads[h], kbuf[pl.ds(off, n_keys), pairs[h // 2]]) for h in range(C_HEADS)]
        probs, inv_den = [], []
        for h in range(C_HEADS):
            s = s_ctx[h]
            if local:
                bias = [bias_ref[h, pl.ds(dr0 + 2 * t, 1)][0] for t in range(NA_WIN_R // 2)]
                s = jnp.concatenate([s, s_loc[h] + jnp.concatenate(bias, axis=-1)], axis=-1)
            p = jnp.exp(s - jnp.max(s, axis=-1, keepdims=True))
            inv_den.append(1.0 / jnp.sum(p, axis=-1, keepdims=True))
            probs.append(p.astype(BF16))
        n_ctx = kx_ref.shape[0]
        o_heads = []
        for h in range(C_HEADS):
            o = _dot(probs[h][:, :n_ctx], vx_ref[:, pairs[h // 2]])
            if local:
                o = o + _dot(probs[h][:, n_ctx:], vbuf[pl.ds(off, n_keys), pairs[h // 2]])
            o_heads.append(o * inv_den[h])
        outs = [jnp.where(first, o_heads[2 * p], o_heads[2 * p + 1]) for p in range(len(pairs))]
        o_ref[pl.ds(pl.multiple_of(rr * W, W), W), :] = jnp.concatenate(outs, axis=-1).astype(o_ref.dtype)

    if local:
        j = pl.program_id(1)
        blk_tokens = rows_per_step * W
        for t, (kr, vr) in enumerate(((kp_ref, vp_ref), (kc_ref, vc_ref), (kn_ref, vn_ref))):
            kbuf[t * blk_tokens:(t + 1) * blk_tokens, :] = kr[...]
            vbuf[t * blk_tokens:(t + 1) * blk_tokens, :] = vr[...]

        def body(rr, carry):
            r = j * rows_per_step + rr
            rs = jnp.clip(r - NA_WIN_R // 2, 0, n_img_rows - NA_WIN_R)
            off = pl.multiple_of((rs - (j - 1) * rows_per_step) * W, W)
            one_row(rr, off, rs - r + NA_WIN_R - 1)
            return carry

        lax.fori_loop(0, rows_per_step, body, 0)
    else:
        def body(rr, carry):
            one_row(rr, None, None)
            return carry

        lax.fori_loop(0, rows_per_step, body, 0)


def _na_bias_table(rpb):
    W = GRID_W
    qcol = jnp.arange(W)
    kcol = jnp.arange(W)
    cstart = jnp.clip(qcol - NA_WIN_C // 2, 0, W - NA_WIN_C)
    col_in = (kcol[None, :] >= cstart[:, None]) & (kcol[None, :] < cstart[:, None] + NA_WIN_C)
    dc_idx = jnp.clip(kcol[None, :] - qcol[:, None] + NA_WIN_C - 1, 0, 2 * NA_WIN_C - 2)
    t2 = rpb.astype(F32)[:, :, dc_idx]
    t2 = jnp.where(col_in[None, None], t2, NEG_INF)
    return jnp.concatenate([t2[:, :-1], t2[:, 1:]], axis=-1)


def _neighborhood_attention(qkv, rpb, B, S, n_ctx_per):
    hw = C_HEADS * C_HEAD_DIM
    R = S // GRID_W
    rows = NA_ROWS
    blk = rows * GRID_W
    nj = R // rows
    ctx_blk0 = (B * S) // n_ctx_per
    kx = pl.BlockSpec((n_ctx_per, hw), lambda b, j: (ctx_blk0 + b, 1))
    vx = pl.BlockSpec((n_ctx_per, hw), lambda b, j: (ctx_blk0 + b, 2))
    bias = _na_bias_table(rpb)

    def nbr(col, delta):
        return pl.BlockSpec((blk, hw), lambda b, j: (b * nj + jnp.clip(j + delta, 0, nj - 1), col))

    lat = pl.pallas_call(
        functools.partial(_nattn_body, n_img_rows=R, local=True),
        grid=(B, nj),
        in_specs=[pl.BlockSpec((blk, hw), lambda b, j: (b * nj + j, 0)),
                  nbr(1, -1), nbr(1, 0), nbr(1, 1), nbr(2, -1), nbr(2, 0), nbr(2, 1), kx, vx,
                  pl.BlockSpec(bias.shape, lambda b, j: (0, 0, 0, 0))],
        out_specs=pl.BlockSpec((blk, hw), lambda b, j: (b * nj + j, 0)),
        out_shape=jax.ShapeDtypeStruct((B * S, hw), BF16),
        scratch_shapes=[pltpu.VMEM((3 * blk, hw), BF16), pltpu.VMEM((3 * blk, hw), BF16)],
        compiler_params=_cparams(("arbitrary", "arbitrary")),
        name="neighborhood_attn",
    )(qkv, qkv, qkv, qkv, qkv, qkv, qkv, qkv, qkv, bias)
    q0 = (B * S) // n_ctx_per
    ctx = pl.pallas_call(
        functools.partial(_nattn_body, n_img_rows=R, local=False),
        grid=(B, 1),
        in_specs=[pl.BlockSpec((n_ctx_per, hw), lambda b, j: (q0 + b, 0)), kx, vx],
        out_specs=pl.BlockSpec((n_ctx_per, hw), lambda b, j: (b, 0)),
        out_shape=jax.ShapeDtypeStruct((B * n_ctx_per, hw), BF16),
        compiler_params=_cparams(("arbitrary", "arbitrary")),
        name="neighborhood_attn_ctx",
    )(qkv, qkv, qkv)
    return lat, ctx


def _log_sigmoid(x):
    return jnp.minimum(x, 0.0) - jnp.log1p(jnp.exp(-jnp.abs(x)))


def _mlstm_body(qf, kf, vf, gcf, grf, qb, kb, vb, gcb, grb, bgr_ref, bgc_ref, hf_ref, hb_ref, st_ref, m_ref):
    ch = qf.shape[0]
    dk, dv = M_QK_DIM, M_V_DIM
    H = M_HEADS

    @pl.when(pl.program_id(1) == 0)
    def _():
        st_ref[...] = jnp.zeros_like(st_ref)
        m_ref[...] = jnp.full_like(m_ref, NEG_INF)

    row_i = lax.broadcasted_iota(jnp.int32, (ch, ch), 0)
    col_i = lax.broadcasted_iota(jnp.int32, (ch, ch), 1)
    lower = row_i >= col_i
    upper = row_i <= col_i
    tril = lower.astype(F32)
    triu = upper.astype(F32)
    hi = lax.Precision.HIGHEST
    dirs = ((qf, kf, vf, gcf, grf, hf_ref), (qb, kb, vb, gcb, grb, hb_ref))
    chains = []
    for direction, (q_ref, k_ref, v_ref, gc_ref, gr_ref, o_ref) in enumerate(dirs):
        fwd = direction == 0
        gcol = gc_ref[...] + bgr_ref[...]
        grow = gr_ref[...] + bgc_ref[...]
        cum_c = jnp.dot(tril if fwd else triu, _log_sigmoid(gcol), precision=hi, preferred_element_type=F32)
        cum_r = jnp.dot(_log_sigmoid(grow), triu if fwd else tril, precision=hi, preferred_element_type=F32)
        for h in range(H):
            c = direction * H + h
            q = q_ref[:, h * dk:(h + 1) * dk].astype(BF16)
            k_s = k_ref[:, h * dk:(h + 1) * dk] * (dk ** -0.5)
            state = st_ref[c]
            chains.append(dict(c=c, h=h, fwd=fwd, gcol=gcol, grow=grow, cum_c=cum_c, cum_r=cum_r, state=state,
                               k_t=k_s.T.astype(BF16), v=v_ref[:, h * dv:(h + 1) * dv], o_ref=o_ref,
                               s_raw=_dot_nt(q, k_s.astype(BF16)), q_state=_dot(q, state.astype(BF16))))
    for ck in chains:
        h, fwd = ck["h"], ck["fwd"]
        gi = (0 if fwd else 2 * H) + h
        gf = gi + H
        b_col = ck["cum_c"][:, gf:gf + 1]
        b_row = ck["cum_r"][gf:gf + 1, :]
        i_col = ck["gcol"][:, gi:gi + 1]
        i_row = ck["grow"][gi:gi + 1, :]
        b_last = b_row[:, ch - 1:ch] if fwd else b_row[:, 0:1]
        m_prev = m_ref[ck["c"]][:, 0:1]
        log_d = jnp.where(lower if fwd else upper, b_col - b_row + i_row, NEG_INF)
        m_inter = b_col + m_prev
        m_t = jnp.maximum(jnp.max(log_d, axis=-1, keepdims=True), m_inter)
        s = ck["s_raw"] * jnp.exp(log_d - m_t)
        m_new = jnp.maximum(b_last + m_prev, jnp.max(b_last - b_row + i_row, axis=-1, keepdims=True))
        w_col = jnp.exp(b_last - b_col + i_col - m_new)
        v = ck["v"]
        wv = jnp.concatenate([v * w_col, jnp.broadcast_to(w_col, (ch, LANES))], axis=-1).astype(BF16)
        ck.update(m_t=m_t, inter=jnp.exp(m_inter - m_t), m_new=m_new, decay=jnp.exp(b_last + m_prev - m_new),
                  s_sum=jnp.sum(s, axis=-1, keepdims=True), sv=_dot(s.astype(BF16), v.astype(BF16)),
                  upd=_dot(ck["k_t"], wv))
    for ck in chains:
        h, c = ck["h"], ck["c"]
        num = ck["sv"] + ck["inter"] * ck["q_state"][:, :dv]
        den = ck["s_sum"] + ck["inter"] * ck["q_state"][:, dv:dv + 1]
        ck["o_ref"][:, h * dv:(h + 1) * dv] = num * (1.0 / jnp.maximum(jnp.abs(den), jnp.exp(-ck["m_t"])))
        st_ref[c] = ck["decay"] * ck["state"] + ck["upd"]
        m_ref[c] = jnp.broadcast_to(ck["m_new"], (1, LANES))


def _mlstm_scan(p, gates_t, b_gate, B, S, n_ctx_per):
    T = p.shape[0]
    ch = MLSTM_CHUNK
    H = M_HEADS
    vw = H * M_V_DIM
    ncc = n_ctx_per // ch
    nlc = S // ch
    gate_col = (2 * H * M_QK_DIM + 2 * vw) // LANES

    def fwd_idx(b, s):
        return jnp.where(s < ncc, (B * S + b * n_ctx_per) // ch + s, b * nlc + s - ncc)

    def bwd_idx(b, s):
        return jnp.where(s < ncc, (B * S + b * n_ctx_per) // ch + ncc - 1 - s, b * nlc + nlc - 1 - (s - ncc))

    def specs(idx):
        return [pl.BlockSpec((ch, H * M_QK_DIM), lambda b, s: (idx(b, s), 0)),
                pl.BlockSpec((ch, H * M_QK_DIM), lambda b, s: (idx(b, s), 1)),
                pl.BlockSpec((ch, vw), lambda b, s: (idx(b, s), 1)),
                pl.BlockSpec((ch, LANES), lambda b, s: (idx(b, s), gate_col)),
                pl.BlockSpec((4 * H, ch), lambda b, s: (0, idx(b, s)))]

    bg_row = jnp.zeros((1, LANES), F32).at[0, :4 * H].set(b_gate.astype(F32))
    bg_col = b_gate.astype(F32).reshape(4 * H, 1)
    return pl.pallas_call(
        _mlstm_body,
        grid=(B, ncc + nlc),
        in_specs=specs(fwd_idx) + specs(bwd_idx) + [pl.BlockSpec((1, LANES), lambda b, s: (0, 0)),
                                                     pl.BlockSpec((4 * H, 1), lambda b, s: (0, 0))],
        out_specs=[pl.BlockSpec((ch, vw), lambda b, s: (fwd_idx(b, s), 0)),
                   pl.BlockSpec((ch, vw), lambda b, s: (bwd_idx(b, s), 0))],
        out_shape=[jax.ShapeDtypeStruct((T, vw), F32)] * 2,
        scratch_shapes=[pltpu.VMEM((2 * H, M_QK_DIM, M_V_DIM + LANES), F32),
                        pltpu.VMEM((2 * H, 1, LANES), F32)],
        compiler_params=_cparams(("arbitrary", "arbitrary")),
        name="mlstm_scan",
    )(p, p, p, p, gates_t, p, p, p, p, gates_t, bg_row, bg_col)


def _route(logits, tril_ref, carry_ref):
    tm = logits.shape[0]
    lane = lax.broadcasted_iota(jnp.int32, (tm, LANES), 1).astype(F32)
    x = jnp.where(lane < N_EXPERTS, logits, -jnp.inf)
    vals, idxs, sels = [], [], []
    for _ in range(TOP_K):
        m = jnp.max(x, axis=-1, keepdims=True)
        idx = jnp.minimum(jnp.min(jnp.where(x == m, lane, float(LANES)), axis=-1, keepdims=True), N_EXPERTS - 1.0)
        sel = lane == idx
        x = jnp.where(sel, -jnp.inf, x)
        vals.append(m)
        idxs.append(idx)
        sels.append(sel)
    e = [jnp.exp(v - vals[0]) for v in vals]
    inv = 1.0 / functools.reduce(lambda a, b: a + b, e)
    onehot = functools.reduce(lambda a, b: a + b, [s.astype(F32) for s in sels])
    before = _dot(tril_ref[...], onehot.astype(BF16)) + carry_ref[...]
    ranks = [jnp.sum(jnp.where(s, before, 0.0), axis=-1, keepdims=True) for s in sels]
    carry_ref[...] = carry_ref[...] + jnp.sum(onehot, axis=0, keepdims=True)
    return (jnp.concatenate(idxs, axis=-1).astype(jnp.int32), jnp.concatenate([g * inv for g in e], axis=-1),
            jnp.concatenate(ranks, axis=-1).astype(jnp.int32))


def _outproj_body(*refs, mlstm, n_lat_tiles):
    tril_ref, xo_ref, h2_ref, idx_ref, gate_ref, rank_ref, cnt_ref, carry_ref = refs[-8:]
    refs = refs[:-8]

    @pl.when(pl.program_id(0) == 0)
    def _():
        carry_ref[...] = jnp.zeros_like(carry_ref)

    if mlstm:
        hf_ref, hb_ref, og_ref, hn_ref, x_ref, mod_ref, g2_ref, wo_ref, rw_ref, rb_ref = refs
        dv = M_V_DIM
        hsum = hf_ref[...] + hb_ref[...]
        parts = []
        for h in range(M_HEADS):
            t = hsum[:, h * dv:(h + 1) * dv]
            parts.append(t * lax.rsqrt(jnp.mean(t * t, axis=-1, keepdims=True) + NORM_EPS))
        a = (jnp.concatenate(parts, axis=-1) * hn_ref[...] * jax.nn.sigmoid(og_ref[...])).astype(BF16)
    else:
        a_lat_ref, a_ctx_ref, x_ref, mod_ref, g2_ref, wo_ref, rw_ref, rb_ref = refs
        a = jnp.where(pl.program_id(0) < n_lat_tiles, a_lat_ref[...], a_ctx_ref[...])
    mod = mod_ref[0]
    x = x_ref[...] + mod[2:3, :] * _dot(a, wo_ref[...])
    xo_ref[...] = x
    h2 = _norm_mod(x, g2_ref[...], mod, 3)
    h2_hi = h2.astype(BF16)
    h2_ref[...] = h2_hi
    h2_lo = (h2 - h2_hi.astype(F32)).astype(BF16)
    t = _dot(h2_hi, rw_ref[...])
    logits = t[:, :LANES] + (t[:, LANES:] + _dot(h2_lo, rw_ref[:, :LANES])) + rb_ref[...]
    idx_ref[...], gate_ref[...], rank_ref[...] = _route(logits, tril_ref, carry_ref)
    cnt_ref[...] = carry_ref[...]


def _out_project(mixer_out, x, mods, g2, wo, router_w, router_b, geom, mlstm):
    T, D = x.shape
    tm, n_lat_tiles, tiles_per_batch, B = geom
    mod_idx = lambda i: jnp.where(i < n_lat_tiles, i // tiles_per_batch, B)
    row = pl.BlockSpec((tm, D), lambda i: (i, 0))
    vec = pl.BlockSpec((1, D), lambda i: (0, 0))
    rw = jnp.zeros((D, LANES), F32).at[:, :N_EXPERTS].set(router_w.astype(F32))
    rw_hi = rw.astype(BF16)
    rw = jnp.concatenate([rw_hi, (rw - rw_hi.astype(F32)).astype(BF16)], axis=1)
    rb = jnp.zeros((1, LANES), F32).at[0, :N_EXPERTS].set(router_b.astype(F32))
    if mlstm:
        hf, hb, p, hnorm = mixer_out
        lead_specs = [row, row, pl.BlockSpec((tm, D), lambda i: (i, 2)), vec]
        lead = [hf, hb, p, hnorm.reshape(1, D)]
    else:
        lead_specs = [pl.BlockSpec((tm, D), lambda i: (jnp.minimum(i, n_lat_tiles - 1), 0)),
                      pl.BlockSpec((tm, D), lambda i: (jnp.maximum(i - n_lat_tiles, 0), 0))]
        lead = list(mixer_out)
    strict_lower = jnp.tril(jnp.ones((tm, tm), BF16), -1)
    topk = pl.BlockSpec((tm, TOP_K), lambda i: (i, 0))
    return pl.pallas_call(
        functools.partial(_outproj_body, mlstm=mlstm, n_lat_tiles=n_lat_tiles),
        grid=(T // tm,),
        in_specs=lead_specs + [row, pl.BlockSpec((1, N_MOD, D), lambda i: (mod_idx(i), 0, 0)), vec,
                               pl.BlockSpec((D, D), lambda i: (0, 0)),
                               pl.BlockSpec((D, 2 * LANES), lambda i: (0, 0)),
                               pl.BlockSpec((1, LANES), lambda i: (0, 0)),
                               pl.BlockSpec((tm, tm), lambda i: (0, 0))],
        out_specs=[row, row, topk, topk, topk, pl.BlockSpec((1, LANES), lambda i: (0, 0))],
        out_shape=[jax.ShapeDtypeStruct((T, D), F32), jax.ShapeDtypeStruct((T, D), BF16),
                   jax.ShapeDtypeStruct((T, TOP_K), jnp.int32), jax.ShapeDtypeStruct((T, TOP_K), F32),
                   jax.ShapeDtypeStruct((T, TOP_K), jnp.int32), jax.ShapeDtypeStruct((1, LANES), F32)],
        scratch_shapes=[pltpu.VMEM((1, LANES), F32)],
        compiler_params=_cparams(("arbitrary",)),
        name="out_proj_norm_router",
    )(*lead, x, mods, g2.reshape(1, D), wo, rw, rb, strict_lower)


def _moe_body(be_ref, nused_ref, x_ref, win_ref, wout_ref, bin_ref, bout_ref, o_ref, win_b, wout_b, stage, act_scr):
    i = pl.program_id(0)
    D, F2 = win_b.shape
    F = F2 // 2
    rows = x_ref.shape[0]
    used = i < nused_ref[0]
    changed = jnp.logical_or(i == 0, be_ref[i] != be_ref[jnp.maximum(i - 1, 0)])

    @pl.when(jnp.logical_and(used, changed))
    def _():
        def cast_rows(r, carry):
            rs = pl.ds(pl.multiple_of(r * LANES, LANES), LANES)
            win_b[rs, :] = win_ref[0, rs, :].astype(BF16)
            return carry

        lax.fori_loop(0, D // LANES, cast_rows, 0)
        for c in range(D // LANES):
            cs = slice(c * LANES, (c + 1) * LANES)
            stage[c, pl.ds(0, F // 2, stride=2), :] = wout_ref[0, :F // 2, cs]
            stage[c, pl.ds(1, F // 2, stride=2), :] = wout_ref[0, F // 2:, cs]
            wout_b[:, cs] = stage[c].astype(BF16)

    @pl.when(used)
    def _():
        x = x_ref[...]
        even = (lax.broadcasted_iota(jnp.int32, (rows, LANES), 1) % 2) == 0
        cw = 256
        for c0 in range(0, F, cw):
            ha = _dot(x, win_b[:, c0:c0 + cw]) + bin_ref[0, :, c0:c0 + cw]
            hb = _dot(x, win_b[:, F + c0:F + c0 + cw]) + bin_ref[0, :, F + c0:F + c0 + cw]
            for l0 in range(0, cw, LANES):
                a = ha[:, l0:l0 + LANES]
                b = hb[:, l0:l0 + LANES]
                gate = jnp.minimum(jnp.where(even, a, pltpu.roll(b, 1, 1)), SWIGLU_LIMIT)
                up = jnp.clip(jnp.where(even, pltpu.roll(a, LANES - 1, 1), b), -SWIGLU_LIMIT, SWIGLU_LIMIT)
                act = (up + 1.0) * (gate * jax.nn.sigmoid(SWIGLU_ALPHA * gate))
                act_scr[:, c0 + l0:c0 + l0 + LANES] = act.astype(BF16)
        o_ref[...] = (_dot(act_scr[...], wout_b[...]) + bout_ref[0]).astype(o_ref.dtype)

    @pl.when(jnp.logical_not(used))
    def _():
        o_ref[...] = jnp.zeros_like(o_ref)


def _moe_experts(xb, block_expert, n_used, layer, w_in, b_in, w_out, b_out):
    n_rows, D = xb.shape
    n_blocks = n_rows // MOE_BLOCK
    L, E, _, F2 = w_in.shape
    F = F2 // 2
    wspec = lambda shape: pl.BlockSpec((1,) + shape, lambda i, be, nu: (layer * E + be[i], 0, 0))
    grid_spec = pltpu.PrefetchScalarGridSpec(
        num_scalar_prefetch=2,
        grid=(n_blocks,),
        in_specs=[pl.BlockSpec((MOE_BLOCK, D), lambda i, be, nu: (i, 0)),
                  wspec((D, F2)), wspec((F, D)), wspec((1, F2)), wspec((1, D))],
        out_specs=pl.BlockSpec((MOE_BLOCK, D), lambda i, be, nu: (i, 0)),
        scratch_shapes=[pltpu.VMEM((D, F2), BF16), pltpu.VMEM((F, D), BF16),
                        pltpu.VMEM((D // LANES, F, LANES), F32), pltpu.VMEM((MOE_BLOCK, F), BF16)],
    )
    return pl.pallas_call(
        _moe_body,
        grid_spec=grid_spec,
        out_shape=jax.ShapeDtypeStruct((n_rows, D), F32),
        compiler_params=_cparams(("arbitrary",)),
        name="moe_experts",
    )(block_expert, n_used, xb, w_in.reshape(L * E, D, F2), w_out.reshape(L * E, F, D),
      b_in.reshape(L * E, 1, F2), b_out.reshape(L * E, 1, D))


COMBINE_TILE = 256
COMBINE_UNROLL = 8


def _combine_body(idx_ref, idx_next_ref, gate_ref, x_ref, mod_ref, yb_hbm, o_ref, buf, sem):
    i = pl.program_id(0)
    n = pl.num_programs(0)
    tm = x_ref.shape[0]
    n_rows = TOP_K * tm
    slot = i % 2

    def fetch_rows(rows_ref, into):
        def body(j, carry):
            for u in range(COMBINE_UNROLL):
                r = j * COMBINE_UNROLL + u
                pltpu.make_async_copy(yb_hbm.at[pl.ds(rows_ref[r], 1)], buf.at[into, pl.ds(r, 1)],
                                      sem.at[into]).start(priority=u % 2)
            return carry

        lax.fori_loop(0, n_rows // COMBINE_UNROLL, body, 0)

    @pl.when(i == 0)
    def _():
        fetch_rows(idx_ref, 0)

    @pl.when(i + 1 < n)
    def _():
        fetch_rows(idx_next_ref, 1 - slot)

    pltpu.make_async_copy(yb_hbm.at[pl.ds(0, n_rows)], buf.at[slot], sem.at[slot]).wait()
    gates = gate_ref[...]
    y = gates[:, 0:1] * buf[slot, 0:tm, :]
    for k in range(1, TOP_K):
        y = y + gates[:, k:k + 1] * buf[slot, k * tm:(k + 1) * tm, :]
    o_ref[...] = x_ref[...] + mod_ref[0][5:6, :] * y


def _moe_combine(yb, dest, gates, x, mods, geom):
    T, D = x.shape
    _, n_lat_tiles, tiles_per_batch, B = geom
    tm = COMBINE_TILE
    per = geom[0] // tm
    mod_idx = lambda i: jnp.where(i // per < n_lat_tiles, (i // per) // tiles_per_batch, B)
    n_tiles = T // tm
    rows = dest.reshape(n_tiles, tm, TOP_K).transpose(0, 2, 1).reshape(-1).astype(jnp.int32)
    nr = TOP_K * tm
    return pl.pallas_call(
        _combine_body,
        grid=(n_tiles,),
        in_specs=[pl.BlockSpec((nr,), lambda i: (i,), memory_space=pltpu.SMEM),
                  pl.BlockSpec((nr,), lambda i: (jnp.minimum(i + 1, n_tiles - 1),), memory_space=pltpu.SMEM),
                  pl.BlockSpec((tm, TOP_K), lambda i: (i, 0)),
                  pl.BlockSpec((tm, D), lambda i: (i, 0)),
                  pl.BlockSpec((1, N_MOD, D), lambda i: (mod_idx(i), 0, 0)),
                  pl.BlockSpec(memory_space=pl.ANY)],
        out_specs=pl.BlockSpec((tm, D), lambda i: (i, 0)),
        out_shape=jax.ShapeDtypeStruct((T, D), F32),
        scratch_shapes=[pltpu.VMEM((2, nr, D), F32), pltpu.SemaphoreType.DMA((2,))],
        compiler_params=_cparams(("arbitrary",)),
        name="moe_combine",
    )(rows, rows, gates, x, mods, yb)


def _moe_layer(xs, h2, routing, mods, geom, layer, w_in, b_in, w_out, b_out):
    T, D = h2.shape
    top_idx, gates, rank, counts = routing
    counts = counts[0, :N_EXPERTS].astype(jnp.int32)
    starts = jnp.cumsum(counts) - counts
    padded = (counts + MOE_BLOCK - 1) // MOE_BLOCK * MOE_BLOCK
    pad_ends = jnp.cumsum(padded)
    pad_starts = pad_ends - padded
    dest = pad_starts[top_idx] + rank
    n_assign = T * TOP_K
    n_blocks = -(-n_assign // MOE_BLOCK) + N_EXPERTS
    n_rows = n_blocks * MOE_BLOCK
    block_start = jnp.arange(n_blocks, dtype=jnp.int32) * MOE_BLOCK
    block_expert = jnp.minimum(jnp.sum((pad_ends[None, :] <= block_start[:, None]).astype(jnp.int32), axis=1),
                               N_EXPERTS - 1)
    n_used = (pad_ends[-1] // MOE_BLOCK).astype(jnp.int32).reshape(1)
    order = jnp.argsort(top_idx.reshape(-1), stable=True).astype(jnp.int32)
    row_e = jnp.repeat(block_expert, MOE_BLOCK)
    row_id = jnp.arange(n_rows, dtype=jnp.int32)
    off = row_id - pad_starts[row_e].astype(jnp.int32)
    src_sorted = jnp.clip(starts[row_e].astype(jnp.int32) + off, 0, n_assign - 1)
    src_tok = jnp.where(off < counts[row_e], order[src_sorted] // TOP_K, row_id % T)
    xb = h2.at[src_tok].get(mode='promise_in_bounds')
    yb = _moe_experts(xb, block_expert, n_used, layer, w_in, b_in, w_out, b_out)
    return _moe_combine(yb, dest, gates, xs, mods, geom)


def _final_norm_body(x_ref, g_ref, o_ref):
    x = x_ref[...]
    o_ref[...] = x * lax.rsqrt(jnp.mean(x * x, axis=-1, keepdims=True) + NORM_EPS) * g_ref[...]


def _final_norm(x, g, n_rows, tm):
    D = x.shape[1]
    return pl.pallas_call(
        _final_norm_body,
        grid=(n_rows // tm,),
        in_specs=[pl.BlockSpec((tm, D), lambda i: (i, 0)), pl.BlockSpec((1, D), lambda i: (0, 0))],
        out_specs=pl.BlockSpec((tm, D), lambda i: (i, 0)),
        out_shape=jax.ShapeDtypeStruct((n_rows, D), F32),
        compiler_params=_cparams(("arbitrary",)),
        name="final_norm",
    )(x, g.reshape(1, D))


def kernel(x, c, ctx, c_ctx, ada_w, ada_b, norm1_g, norm2_g, a_wqkv, a_wo, a_sink, m_win, m_bgate, m_hnorm,
           m_wo, n_wqkv, n_wo, n_rpb, router_w, router_b, exp_w_in, exp_b_in, exp_w_out, exp_b_out, final_g):
    B, S, D = x.shape
    Lc = ctx.shape[1]
    depth = ada_w.shape[0]
    n_lat, n_ctx = B * S, B * Lc
    assert D == D_MODEL and S % (NA_ROWS * GRID_W) == 0 and Lc % MLSTM_CHUNK == 0 and n_lat % Lc == 0
    tm = _row_tile(S, n_ctx)
    geom = (tm, n_lat // tm, S // tm, B)

    xs = jnp.concatenate([x.reshape(n_lat, D), ctx.reshape(n_ctx, D)], axis=0).astype(F32)
    c_rows = -(-(B + 1) // 8) * 8
    c_all = jnp.zeros((c_rows, D), F32).at[:B].set(c.astype(F32)).at[B].set(c_ctx.astype(F32))
    mods = _ada_mods(c_all, ada_w.astype(F32), ada_b.astype(F32)).reshape(depth, c_rows, N_MOD, D)
    rope = _rope_tables(B, S, n_ctx)

    for layer in range(depth):
        kind = layer % N_MIXERS
        j = layer // N_MIXERS
        mod_l = mods[layer]
        g1 = norm1_g[layer].astype(F32)
        if kind == 0:
            rope_cols = (A_HEADS + A_KV_HEADS) * A_HEAD_DIM
            qkv = _project(xs, mod_l, g1, a_wqkv[j].astype(BF16), geom, BF16, rope, rope_cols // LANES)
            mixed = _window_attention(qkv, a_sink[j].astype(F32), B, S, Lc)
            wo = a_wo[j]
        elif kind == 1:
            n_in = m_win.shape[2]
            n_pad = -(-n_in // LANES) * LANES
            w_in = jnp.zeros((D, n_pad), BF16).at[:, :n_in].set(m_win[j].astype(BF16))
            p = _project(xs, mod_l, g1, w_in, geom, F32)
            gates_t = p[:, n_in - 4 * M_HEADS:n_in].T
            hf, hb = _mlstm_scan(p, gates_t, m_bgate[j], B, S, Lc)
            mixed = (hf, hb, p, m_hnorm[j].astype(F32))
            wo = m_wo[j]
        else:
            qkv = _project(xs, mod_l, g1, n_wqkv[j].astype(BF16), geom, BF16)
            mixed = _neighborhood_attention(qkv, n_rpb[j], B, S, Lc)
            wo = n_wo[j]
        xs, h2, *routing = _out_project(mixed, xs, mod_l, norm2_g[layer].astype(F32), wo.astype(BF16),
                                        router_w[layer], router_b[layer], geom, kind == 1)
        xs = _moe_layer(xs, h2, routing, mod_l, geom, layer, exp_w_in, exp_b_in, exp_w_out, exp_b_out)
    return _final_norm(xs, final_g.astype(F32), n_lat, tm).reshape(B, S, D).astype(x.dtype)
```
